```python
import math
import jax, jax.numpy as jnp
from jax import lax
import numpy as np

D_MODEL = 1024
BATCH = 4
SEQ = 8192
DEPTH = 2
DEC_BATCH = 8
DEC_SEQ = 4096
PAST_LEN = 128

CONV_CH = 256
CONV_WIDTH = 31
SSM_CH = 256
SSM_GROUP = 16
SSM_GROUPS = SSM_CH // SSM_GROUP
SSM_STATE = 64
MLA_HEADS = 4
MLA_NOPE = 128
MLA_ROPE = 64
MLA_V = 128
MLA_DK = MLA_NOPE + MLA_ROPE
MLA_Q_RANK = 512
MLA_KV_RANK = 256
MLA_WIDTH = MLA_HEADS * MLA_V
MIX_WIDTH = CONV_CH + SSM_CH + MLA_WIDTH
IN_WIDTHS = (2 * CONV_CH, SSM_CH, MLA_Q_RANK, MLA_KV_RANK, MLA_ROPE)
IN_WIDTH = sum(IN_WIDTHS)
IN_SPLITS = tuple(int(s) for s in np.cumsum(IN_WIDTHS)[:-1])
ROPE_THETA = 10000.0
Q_BLOCK = 128
PEER_HEADS = 8
PEER_NKEYS = 128
PEER_EXPERTS = PEER_NKEYS * PEER_NKEYS
PEER_DQ = 256
PEER_HALF = PEER_DQ // 2
PEER_TOPK = 16
PEER_CHUNK = 128
PLE_DIM = 256
DEEPNORM_ALPHA = float((2 * DEPTH) ** 0.25)
DEEPNORM_BETA = float((8 * DEPTH) ** -0.25)
LN_EPS = 1e-5
RMS_EPS = 1e-6

kernel_name = "hymba_style_conv_s5_mla_peer_encoder"

F32 = jnp.float32


def layer_norm(x, g, b):
    xf = x.astype(F32)
    mu = jnp.mean(xf, -1, keepdims=True)
    var = jnp.mean(jnp.square(xf - mu), -1, keepdims=True)
    return ((xf - mu) * lax.rsqrt(var + LN_EPS) * g.astype(F32) + b.astype(F32)).astype(x.dtype)


def rms_norm(x, g):
    xf = x.astype(F32)
    return (xf * lax.rsqrt(jnp.mean(jnp.square(xf), -1, keepdims=True) + RMS_EPS) * g.astype(F32)).astype(x.dtype)


def rope(x, cos, sin):
    x1, x2 = jnp.split(x.astype(F32), 2, axis=-1)
    return jnp.concatenate([x1 * cos - x2 * sin, x1 * sin + x2 * cos], -1).astype(x.dtype)


def conv_mixer(z, conv_w, conv_b, ln_g, ln_b):
    a, gate = jnp.split(z, 2, axis=-1)
    h = a * jax.nn.sigmoid(gate)
    h = lax.conv_general_dilated(
        h, conv_w[:, None, :], window_strides=(1,),
        padding=[(CONV_WIDTH // 2, CONV_WIDTH // 2)],
        dimension_numbers=("NWC", "WIO", "NWC"),
        feature_group_count=CONV_CH) + conv_b
    return jax.nn.silu(layer_norm(h, ln_g, ln_b))


def _ssm_combine(e1, e2):
    a1, b1 = e1
    a2, b2 = e2
    return a1 * a2, a2 * b1 + b2


def ssm_mixer(u, a_re, a_im, b_re, b_im, c_re, c_im, log_dt, d, glu_w, glu_b):
    bsz, L, _ = u.shape
    uf = u.astype(F32)
    uc = uf.reshape(bsz, L, SSM_GROUPS, SSM_GROUP).astype(jnp.complex64)
    A = lax.complex(a_re.astype(F32), a_im.astype(F32))
    dt = jnp.exp(log_dt.astype(F32))[..., None]
    a_bar = jnp.exp(dt * A)
    Bm = lax.complex(b_re.astype(F32), b_im.astype(F32))
    b_bar = ((a_bar - 1.0) / A)[..., None] * Bm
    Cm = lax.complex(c_re.astype(F32), c_im.astype(F32))
    y = d.astype(F32) * uf
    for direction, reverse in ((0, False), (1, True)):
        bu = jnp.einsum("blgc,gpc->blgp", uc, b_bar[direction])
        a = jnp.broadcast_to(a_bar[direction], bu.shape)
        _, hs = lax.associative_scan(_ssm_combine, (a, bu), axis=1, reverse=reverse)
        y = y + jnp.real(jnp.einsum("blgp,gcp->blgc", hs, Cm[direction])).reshape(bsz, L, SSM_CH)
    g = jax.nn.gelu(y).astype(u.dtype)
    return g * jax.nn.sigmoid(g @ glu_w + glu_b)


def mla_mixer(cq, ckv, kr, cos, sin, q_norm_g, w_uq, kv_norm_g, w_ukv):
    bsz, L, _ = cq.shape
    q = (rms_norm(cq, q_norm_g) @ w_uq).reshape(bsz, L, MLA_HEADS, MLA_DK)
    q = jnp.concatenate([q[..., :MLA_NOPE], rope(q[..., MLA_NOPE:], cos[:, None, :], sin[:, None, :])], -1)
    kv = (rms_norm(ckv, kv_norm_g) @ w_ukv).reshape(bsz, L, MLA_HEADS, MLA_NOPE + MLA_V)
    k_nope, v = kv[..., :MLA_NOPE], kv[..., MLA_NOPE:]
    k_rope = rope(kr, cos, sin)
    k = jnp.concatenate([k_nope, jnp.broadcast_to(k_rope[:, :, None, :], (bsz, L, MLA_HEADS, MLA_ROPE))], -1)
    q = q * (MLA_DK ** -0.5)
    qb = q.reshape(bsz, L // Q_BLOCK, Q_BLOCK, MLA_HEADS, MLA_DK).transpose(1, 0, 2, 3, 4)

    def attend(q_blk):
        s = jnp.einsum("bqhd,bkhd->bhqk", q_blk, k, preferred_element_type=F32)
        p = jax.nn.softmax(s, axis=-1).astype(v.dtype)
        return jnp.einsum("bhqk,bkhd->bqhd", p, v)

    o = lax.map(attend, qb)
    return o.transpose(1, 0, 2, 3, 4).reshape(bsz, L, MLA_WIDTH)


def peer(x, w_q, k1, k2, u_tab, v_tab):
    bsz, L, D = x.shape
    T = bsz * L
    xt = x.reshape(T, D)
    q = (xt @ w_q).reshape(T, PEER_HEADS, 2, PEER_HALF)
    s1 = jnp.einsum("thd,kd->thk", q[:, :, 0], k1, preferred_element_type=F32)
    s2 = jnp.einsum("thd,kd->thk", q[:, :, 1], k2, preferred_element_type=F32)
    v1, i1 = lax.top_k(s1, PEER_TOPK)
    v2, i2 = lax.top_k(s2, PEER_TOPK)
    cand = (v1[..., :, None] + v2[..., None, :]).reshape(T, PEER_HEADS, PEER_TOPK * PEER_TOPK)
    best, pos = lax.top_k(cand, PEER_TOPK)
    e1 = jnp.take_along_axis(i1, pos // PEER_TOPK, axis=-1)
    e2 = jnp.take_along_axis(i2, pos % PEER_TOPK, axis=-1)
    idx = (e1 * PEER_NKEYS + e2).reshape(T, PEER_HEADS * PEER_TOPK)
    gates = jax.nn.softmax(best, axis=-1).astype(x.dtype).reshape(T, PEER_HEADS * PEER_TOPK)
    n_chunks = T // PEER_CHUNK

    def expert_chunk(args):
        xc, ic, gc = args
        act = jax.nn.gelu(jnp.einsum("td,tkd->tk", xc, u_tab[ic]))
        return jnp.einsum("tk,tkd->td", act * gc, v_tab[ic])

    out = lax.map(expert_chunk, (xt.reshape(n_chunks, PEER_CHUNK, D),
                                 idx.reshape(n_chunks, PEER_CHUNK, -1),
                                 gates.reshape(n_chunks, PEER_CHUNK, -1)))
    return out.reshape(bsz, L, D)


def setup_inputs(seed: int = 0) -> dict:
    key = jax.random.key(seed)
    ks = iter(jax.random.split(key, 48))

    def nrm(shape, scale):
        return jax.random.normal(next(ks), shape, F32) * scale

    def gain(shape):
        return 1.0 + nrm(shape, 0.02)

    G, P = SSM_GROUPS, SSM_STATE
    a_im_init = jnp.broadcast_to(math.pi * jnp.arange(P, dtype=F32), (DEPTH, 2, G, P))
    return {
        "x_prompt": nrm((BATCH, SEQ, D_MODEL), 1.0),
        "x_sample": nrm((DEC_BATCH, DEC_SEQ, D_MODEL), 1.0),
        "p_prompt": nrm((DEPTH, BATCH, SEQ, PLE_DIM), 1.0),
        "p_sample": nrm((DEPTH, DEC_BATCH, DEC_SEQ, PLE_DIM), 1.0),
        "ln_emb_g": gain((D_MODEL,)),
        "ln_emb_b": nrm((D_MODEL,), 0.02),
        "w_in": nrm((DEPTH, D_MODEL, IN_WIDTH), D_MODEL ** -0.5),
        "conv_w": nrm((DEPTH, CONV_WIDTH, CONV_CH), CONV_WIDTH ** -0.5),
        "conv_b": nrm((DEPTH, CONV_CH), 0.02),
        "conv_ln_g": gain((DEPTH, CONV_CH)),
        "conv_ln_b": nrm((DEPTH, CONV_CH), 0.02),
        "ssm_a_re": -0.5 + nrm((DEPTH, 2, G, P), 0.01),
        "ssm_a_im": a_im_init + nrm((DEPTH, 2, G, P), 0.01),
        "ssm_b_re": nrm((DEPTH, 2, G, P, SSM_GROUP), (2 * SSM_GROUP) ** -0.5),
        "ssm_b_im": nrm((DEPTH, 2, G, P, SSM_GROUP), (2 * SSM_GROUP) ** -0.5),
        "ssm_c_re": nrm((DEPTH, 2, G, SSM_GROUP, P), (2 * P) ** -0.5),
        "ssm_c_im": nrm((DEPTH, 2, G, SSM_GROUP, P), (2 * P) ** -0.5),
        "ssm_log_dt": jax.random.uniform(next(ks), (DEPTH, 2, G), F32, math.log(1e-3), math.log(1e-1)),
        "ssm_d": nrm((DEPTH, SSM_CH), 1.0),
        "ssm_glu_w": nrm((DEPTH, SSM_CH, SSM_CH), SSM_CH ** -0.5),
        "ssm_glu_b": nrm((DEPTH, SSM_CH), 0.02),
        "mla_q_norm_g": gain((DEPTH, MLA_Q_RANK)),
        "mla_w_uq": nrm((DEPTH, MLA_Q_RANK, MLA_HEADS * MLA_DK), MLA_Q_RANK ** -0.5),
        "mla_kv_norm_g": gain((DEPTH, MLA_KV_RANK)),
        "mla_w_ukv": nrm((DEPTH, MLA_KV_RANK, MLA_HEADS * (MLA_NOPE + MLA_V)), MLA_KV_RANK ** -0.5),
        "w_out": nrm((DEPTH, MIX_WIDTH, D_MODEL), DEEPNORM_BETA * MIX_WIDTH ** -0.5),
        "ln1_g": gain((DEPTH, D_MODEL)),
        "ln1_b": nrm((DEPTH, D_MODEL), 0.02),
        "peer_w_q": nrm((DEPTH, D_MODEL, PEER_HEADS * PEER_DQ), D_MODEL ** -0.5),
        "peer_k1": nrm((DEPTH, PEER_NKEYS, PEER_HALF), PEER_HALF ** -0.5),
        "peer_k2": nrm((DEPTH, PEER_NKEYS, PEER_HALF), PEER_HALF ** -0.5),
        "peer_u": nrm((DEPTH, PEER_EXPERTS, D_MODEL), D_MODEL ** -0.5),
        "peer_v": nrm((DEPTH, PEER_EXPERTS, D_MODEL), DEEPNORM_BETA),
        "ple_w_p": nrm((DEPTH, PLE_DIM, D_MODEL), DEEPNORM_BETA * PLE_DIM ** -0.5),
        "ple_w_g": nrm((DEPTH, D_MODEL, D_MODEL), D_MODEL ** -0.5),
        "ln2_g": gain((DEPTH, D_MODEL)),
        "ln2_b": nrm((DEPTH, D_MODEL), 0.02),
    }


def reference(x_prompt, x_sample, p_prompt, p_sample, ln_emb_g, ln_emb_b, w_in,
              conv_w, conv_b, conv_ln_g, conv_ln_b,
              ssm_a_re, ssm_a_im, ssm_b_re, ssm_b_im, ssm_c_re, ssm_c_im, ssm_log_dt, ssm_d, ssm_glu_w, ssm_glu_b,
              mla_q_norm_g, mla_w_uq, mla_kv_norm_g, mla_w_ukv,
              w_out, ln1_g, ln1_b,
              peer_w_q, peer_k1, peer_k2, peer_u, peer_v,
              ple_w_p, ple_w_g, ln2_g, ln2_b):

    def run(x, p):
        L = x.shape[1]
        pos = jnp.arange(L, dtype=F32)
        inv_freq = ROPE_THETA ** (-jnp.arange(0, MLA_ROPE, 2, dtype=F32) / MLA_ROPE)
        ang = pos[:, None] * inv_freq[None, :]
        cos, sin = jnp.cos(ang), jnp.sin(ang)
        h = layer_norm(x, ln_emb_g, ln_emb_b)
        for i in range(DEPTH):
            z = h @ w_in[i]
            z_conv, z_ssm, z_cq, z_ckv, z_kr = jnp.split(z, IN_SPLITS, axis=-1)
            o_conv = conv_mixer(z_conv, conv_w[i], conv_b[i], conv_ln_g[i], conv_ln_b[i])
            o_ssm = ssm_mixer(z_ssm, ssm_a_re[i], ssm_a_im[i], ssm_b_re[i], ssm_b_im[i],
                              ssm_c_re[i], ssm_c_im[i], ssm_log_dt[i], ssm_d[i], ssm_glu_w[i], ssm_glu_b[i])
            o_mla = mla_mixer(z_cq, z_ckv, z_kr, cos, sin, mla_q_norm_g[i], mla_w_uq[i],
                              mla_kv_norm_g[i], mla_w_ukv[i])
            mix = jnp.concatenate([o_conv, o_ssm, o_mla], axis=-1) @ w_out[i]
            h = layer_norm(DEEPNORM_ALPHA * h + mix, ln1_g[i], ln1_b[i])
            r = DEEPNORM_ALPHA * h + peer(h, peer_w_q[i], peer_k1[i], peer_k2[i], peer_u[i], peer_v[i])
            r = r + jax.nn.sigmoid(h @ ple_w_g[i]) * (p[i] @ ple_w_p[i])
            h = layer_norm(r, ln2_g[i], ln2_b[i])
        return h

    y_prompt = run(x_prompt, p_prompt)
    y_sample = run(x_sample, p_sample)
    return (y_prompt, y_sample)
```

```python
import functools
import math

import jax
import jax.numpy as jnp
import numpy as np
from jax import lax
from jax.experimental import pallas as pl
from jax.experimental.pallas import tpu as pltpu

F32 = jnp.float32
BF16 = jnp.bfloat16

D_MODEL = 1024
DEPTH = 2
CONV_CH = 256
CONV_WIDTH = 31
CONV_HALO = 16
SSM_CH = 256
SSM_GROUP = 16
SSM_GROUPS = 16
SSM_STATE = 64
SSM_LANES = SSM_GROUPS * SSM_STATE
MLA_HEADS = 4
MLA_NOPE = 128
MLA_ROPE = 64
MLA_V = 128
MLA_DK = MLA_NOPE + MLA_ROPE
MLA_DKP = 256
MLA_Q_RANK = 512
MLA_KV_RANK = 256
ROPE_THETA = 10000.0
PEER_HEADS = 8
PEER_NKEYS = 128
PEER_EXPERTS = PEER_NKEYS * PEER_NKEYS
PEER_HALF = 128
PEER_TOPK = 16
PLE_DIM = 256
DEEPNORM_ALPHA = float((2 * DEPTH) ** 0.25)
LN_EPS = 1e-5
RMS_EPS = 1e-6
IN_PAD = 1664

LANE = 128
SUBLANE = 8
VMEM_LIMIT = 56 * 1024 * 1024

_CAND_N = PEER_TOPK + 1
_CAND_PAIRS = tuple((i, j) for i in range(_CAND_N) for j in range(_CAND_N) if (i + 1) * (j + 1) <= _CAND_N)
_CAND_ROWS = -(-len(_CAND_PAIRS) // SUBLANE) * SUBLANE


def _cparams(sem):
    return pltpu.CompilerParams(dimension_semantics=sem, vmem_limit_bytes=VMEM_LIMIT)


def _sigmoid(x):
    return 1.0 / (1.0 + jnp.exp(-x))


def _gelu(x):
    return 0.5 * x * (1.0 + jnp.tanh(math.sqrt(2.0 / math.pi) * (x + 0.044715 * (x * x * x))))


def _layer_norm(x, g, b):
    mu = jnp.mean(x, axis=-1, keepdims=True)
    xc = x - mu
    var = jnp.mean(xc * xc, axis=-1, keepdims=True)
    return xc * lax.rsqrt(var + LN_EPS) * g + b


def _rms_norm(x, g):
    return x * lax.rsqrt(jnp.mean(x * x, axis=-1, keepdims=True) + RMS_EPS) * g


def _dot(a, b):
    return jnp.dot(a, b, preferred_element_type=F32)


def _dot_nt(a, b):
    return lax.dot_general(a, b, (((1,), (1,)), ((), ())), preferred_element_type=F32)


def _in_proj_kernel(*refs, pre_ln):
    if pre_ln:
        (x_ref, g_ref, b_ref, w_ref, qg_ref, wuq_ref, kvg_ref, wukv_ref, rc_ref, rs1_ref, rs2_ref,
         h_ref, zc_ref, zs_ref, q_ref, k_ref, v_ref) = refs
    else:
        (x_ref, w_ref, qg_ref, wuq_ref, kvg_ref, wukv_ref, rc_ref, rs1_ref, rs2_ref,
         zc_ref, zs_ref, q_ref, k_ref, v_ref) = refs
    x = x_ref[...]
    if pre_ln:
        x = _layer_norm(x, g_ref[...], b_ref[...])
        h_ref[...] = x
    z = _dot(x.astype(BF16), w_ref[...])
    zc_ref[...] = z[:, 0:512]
    zs_ref[...] = z[:, 512:768]
    cq = _rms_norm(z[:, 768:1280], qg_ref[...])
    ckv = _rms_norm(z[:, 1280:1536], kvg_ref[...])
    kr = z[:, 1536:1664]
    q = _dot(cq.astype(BF16), wuq_ref[...])
    kv = _dot(ckv.astype(BF16), wukv_ref[...])
    rc, rs1, rs2 = rc_ref[...], rs1_ref[...], rs2_ref[...]

    def rope(t):
        return t * rc + pltpu.roll(t, 96, 1) * rs1 + pltpu.roll(t, 32, 1) * rs2

    scale = MLA_DK ** -0.5
    k_rope = rope(kr).astype(BF16)
    for h in range(MLA_HEADS):
        base = h * MLA_DKP
        q_ref[h, :, 0:128] = (q[:, base:base + 128] * scale).astype(BF16)
        q_ref[h, :, 128:256] = (rope(q[:, base + 128:base + 256]) * scale).astype(BF16)
        k_ref[h, :, 0:128] = kv[:, h * 128:(h + 1) * 128].astype(BF16)
        k_ref[h, :, 128:256] = k_rope
        v_ref[h] = kv[:, 512 + h * 128:512 + (h + 1) * 128].astype(BF16)


def _in_proj(x, lw, rope_tabs, ln=None):
    B, L, D = x.shape
    TM = min(512, L)
    nt = L // TM
    pre_ln = ln is not None
    row = lambda b, j: (b, j, 0)
    const2 = lambda b, j: (0, 0)
    in_specs = [pl.BlockSpec((None, TM, D), row)]
    args = [x]
    if pre_ln:
        in_specs += [pl.BlockSpec((1, D), const2)] * 2
        args += [ln[0], ln[1]]
    in_specs += [
        pl.BlockSpec((D, IN_PAD), const2),
        pl.BlockSpec((1, MLA_Q_RANK), const2),
        pl.BlockSpec((MLA_Q_RANK, MLA_HEADS * MLA_DKP), const2),
        pl.BlockSpec((1, MLA_KV_RANK), const2),
        pl.BlockSpec((MLA_KV_RANK, 2 * MLA_HEADS * 128), const2),
    ] + [pl.BlockSpec((TM, LANE), lambda b, j: (j, 0))] * 3
    args += [lw["w_in"], lw["q_norm_g"], lw["w_uq"], lw["kv_norm_g"], lw["w_ukv"], *rope_tabs]
    out_shape, out_specs = [], []
    if pre_ln:
        out_shape.append(jax.ShapeDtypeStruct((B, L, D), F32))
        out_specs.append(pl.BlockSpec((None, TM, D), row))
    out_shape += [
        jax.ShapeDtypeStruct((B, L, 2 * CONV_CH), F32),
        jax.ShapeDtypeStruct((L, B * SSM_CH), F32),
        jax.ShapeDtypeStruct((B, MLA_HEADS, L, MLA_DKP), BF16),
        jax.ShapeDtypeStruct((B, MLA_HEADS, L, MLA_DKP), BF16),
        jax.ShapeDtypeStruct((B, MLA_HEADS, L, MLA_V), BF16),
    ]
    hrow = lambda b, j: (b, 0, j, 0)
    out_specs += [
        pl.BlockSpec((None, TM, 2 * CONV_CH), row),
        pl.BlockSpec((TM, SSM_CH), lambda b, j: (j, b)),
        pl.BlockSpec((None, MLA_HEADS, TM, MLA_DKP), hrow),
        pl.BlockSpec((None, MLA_HEADS, TM, MLA_DKP), hrow),
        pl.BlockSpec((None, MLA_HEADS, TM, MLA_V), hrow),
    ]
    return pl.pallas_call(
        functools.partial(_in_proj_kernel, pre_ln=pre_ln),
        grid=(B, nt), in_specs=in_specs, out_specs=out_specs, out_shape=out_shape,
        compiler_params=_cparams(("parallel", "parallel")), name="in_proj",
    )(*args)


def _attn_kernel(q_ref, k_ref, v_ref, o_ref, m_ref, l_ref, acc_ref, *, nk):
    j = pl.program_id(3)

    @pl.when(j == 0)
    def _():
        m_ref[...] = jnp.full(m_ref.shape, -jnp.inf, F32)
        l_ref[...] = jnp.zeros(l_ref.shape, F32)
        acc_ref[...] = jnp.zeros(acc_ref.shape, F32)

    s = _dot_nt(q_ref[...], k_ref[...])
    m_prev = m_ref[...]
    m_new = jnp.maximum(m_prev, jnp.max(s, axis=1, keepdims=True))
    alpha = jnp.exp(m_prev - m_new)
    p = jnp.exp(s - m_new)
    l_ref[...] = alpha * l_ref[...] + jnp.sum(p, axis=1, keepdims=True)
    acc_ref[...] = alpha * acc_ref[...] + _dot(p.astype(BF16), v_ref[...])
    m_ref[...] = m_new

    @pl.when(j == nk - 1)
    def _():
        o_ref[...] = (acc_ref[...] / l_ref[...]).astype(o_ref.dtype)


def _attention(q, k, v):
    B, H, L, _ = q.shape
    TQ = min(1024, L)
    TK = min(1024, L)
    nq, nk = L // TQ, L // TK
    return pl.pallas_call(
        functools.partial(_attn_kernel, nk=nk),
        grid=(B, H, nq, nk),
        in_specs=[
            pl.BlockSpec((None, None, TQ, MLA_DKP), lambda b, h, i, j: (b, h, i, 0)),
            pl.BlockSpec((None, None, TK, MLA_DKP), lambda b, h, i, j: (b, h, j, 0)),
            pl.BlockSpec((None, None, TK, MLA_V), lambda b, h, i, j: (b, h, j, 0)),
        ],
        out_specs=pl.BlockSpec((None, TQ, MLA_V), lambda b, h, i, j: (b, i, h)),
        out_shape=jax.ShapeDtypeStruct((B, L, H * MLA_V), BF16),
        scratch_shapes=[pltpu.VMEM((TQ, 1), F32), pltpu.VMEM((TQ, 1), F32), pltpu.VMEM((TQ, MLA_V), F32)],
        compiler_params=_cparams(("parallel", "parallel", "parallel", "arbitrary")), name="attn",
    )(q, k, v)


def _conv_kernel(zm_ref, zp_ref, zn_ref, w_ref, cb_ref, g_ref, b_ref, o_ref, hp_ref, *, tc, nt, rc):
    j = pl.program_id(1)

    def glu(z):
        return z[:, :CONV_CH] * _sigmoid(z[:, CONV_CH:])

    hp_ref[0:CONV_HALO, :] = jnp.where(j > 0, glu(zp_ref[...]), 0.0)
    hp_ref[CONV_HALO:CONV_HALO + tc, :] = glu(zm_ref[...])
    hp_ref[CONV_HALO + tc:2 * CONV_HALO + tc, :] = jnp.where(j < nt - 1, glu(zn_ref[...]), 0.0)
    shift = CONV_HALO - CONV_WIDTH // 2
    for r in range(tc // rc):
        acc = jnp.zeros((rc, CONV_CH), F32)
        for kk in range(CONV_WIDTH):
            lo = r * rc + kk + shift
            acc = acc + hp_ref[lo:lo + rc, :] * w_ref[kk:kk + 1, :]
        y = _layer_norm(acc + cb_ref[...], g_ref[...], b_ref[...])
        o_ref[r * rc:(r + 1) * rc, :] = (y * _sigmoid(y)).astype(o_ref.dtype)


def _conv(zc, lw):
    B, L, _ = zc.shape
    TC = min(512, L)
    nt = L // TC
    hb = TC // CONV_HALO
    nhb = L // CONV_HALO
    const2 = lambda b, j: (0, 0)
    return pl.pallas_call(
        functools.partial(_conv_kernel, tc=TC, nt=nt, rc=min(64, TC)),
        grid=(B, nt),
        in_specs=[
            pl.BlockSpec((None, TC, 2 * CONV_CH), lambda b, j: (b, j, 0)),
            pl.BlockSpec((None, CONV_HALO, 2 * CONV_CH), lambda b, j: (b, jnp.maximum(j * hb - 1, 0), 0)),
            pl.BlockSpec((None, CONV_HALO, 2 * CONV_CH), lambda b, j: (b, jnp.minimum((j + 1) * hb, nhb - 1), 0)),
            pl.BlockSpec((CONV_WIDTH, CONV_CH), const2),
            pl.BlockSpec((1, CONV_CH), const2),
            pl.BlockSpec((1, CONV_CH), const2),
            pl.BlockSpec((1, CONV_CH), const2),
        ],
        out_specs=pl.BlockSpec((None, TC, CONV_CH), lambda b, j: (b, j, 0)),
        out_shape=jax.ShapeDtypeStruct((B, L, CONV_CH), BF16),
        scratch_shapes=[pltpu.VMEM((TC + 2 * CONV_HALO, CONV_CH), F32)],
        compiler_params=_cparams(("parallel", "parallel")), name="conv",
    )(zc, zc, zc, lw["conv_w"], lw["conv_b"], lw["conv_ln_g"], lw["conv_ln_b"])


def _ssm_kernel(uf_ref, ub_ref, bm_ref, coef_ref, cm_ref, yf_ref, yb_ref,
                sf_ref, sb_ref, hf_ref, hb_ref, *, nb, steps, lc):
    i = pl.program_id(0)
    S = SSM_LANES

    @pl.when(i == 0)
    def _():
        hf_ref[...] = jnp.zeros(hf_ref.shape, F32)
        hb_ref[...] = jnp.zeros(hb_ref.shape, F32)

    sf_ref[...] = _dot(uf_ref[...].astype(BF16), bm_ref[0])
    sb_ref[...] = _dot(ub_ref[...].astype(BF16), bm_ref[1])

    lo_half = lax.broadcasted_iota(jnp.int32, (SUBLANE, lc), 0) < nb

    for c0 in range(0, S, lc):
        re_l = pl.ds(c0, lc)
        im_l = pl.ds(S + c0, lc)
        cf = [coef_ref[0, k, :, c0:c0 + lc] for k in range(4)]
        cb = [coef_ref[1, k, :, c0:c0 + lc] for k in range(4)]

        def body(t, carry, re_l=re_l, im_l=im_l, cf=cf, cb=cb):
            hfr, hfi, hbr, hbi = carry
            rf = pl.ds(pl.multiple_of(t * SUBLANE, SUBLANE), SUBLANE)
            rb = pl.ds(pl.multiple_of((steps - 1 - t) * SUBLANE, SUBLANE), SUBLANE)
            out = []
            for ref, rows, c, hr, hi, fwd in ((sf_ref, rf, cf, hfr, hfi, True), (sb_ref, rb, cb, hbr, hbi, False)):
                xr = ref[rows, re_l]
                xi = ref[rows, im_l]
                nr = c[0] * hr - c[1] * hi + xr
                ni = c[0] * hi + c[1] * hr + xi
                if nb != SUBLANE:
                    sxr = pltpu.roll(xr, nb, 0)
                    sxi = pltpu.roll(xi, nb, 0)
                    nr = nr + (c[2] * sxr - c[3] * sxi)
                    ni = ni + (c[2] * sxi + c[3] * sxr)
                ref[rows, re_l] = nr
                ref[rows, im_l] = ni
                if nb != SUBLANE:
                    snr = pltpu.roll(nr, nb, 0)
                    sni = pltpu.roll(ni, nb, 0)
                    if fwd:
                        nr = jnp.where(lo_half, snr, nr)
                        ni = jnp.where(lo_half, sni, ni)
                    else:
                        nr = jnp.where(lo_half, nr, snr)
                        ni = jnp.where(lo_half, ni, sni)
                out += [nr, ni]
            return tuple(out)

        init = (hf_ref[:, re_l], hf_ref[:, im_l], hb_ref[:, re_l], hb_ref[:, im_l])
        hfr, hfi, hbr, hbi = lax.fori_loop(0, steps, body, init, unroll=2)
        hf_ref[:, re_l] = hfr
        hf_ref[:, im_l] = hfi
        hb_ref[:, re_l] = hbr
        hb_ref[:, im_l] = hbi

    yf_ref[...] = _dot(sf_ref[...].astype(BF16), cm_ref[0])
    yb_ref[...] = _dot(sb_ref[...].astype(BF16), cm_ref[1])


def _ssm(u_rows, lw, nb):
    N = u_rows.shape[0]
    R = min(1024, N)
    n = N // R
    steps = R // SUBLANE
    S2 = 2 * SSM_LANES
    return pl.pallas_call(
        functools.partial(_ssm_kernel, nb=nb, steps=steps, lc=512),
        grid=(n,),
        in_specs=[
            pl.BlockSpec((R, SSM_CH), lambda i: (i, 0)),
            pl.BlockSpec((R, SSM_CH), lambda i: (n - 1 - i, 0)),
            pl.BlockSpec((2, SSM_CH, S2), lambda i: (0, 0, 0)),
            pl.BlockSpec((2, 4, SUBLANE, SSM_LANES), lambda i: (0, 0, 0, 0)),
            pl.BlockSpec((2, S2, SSM_CH), lambda i: (0, 0, 0)),
        ],
        out_specs=[
            pl.BlockSpec((R, SSM_CH), lambda i: (i, 0)),
            pl.BlockSpec((R, SSM_CH), lambda i: (n - 1 - i, 0)),
        ],
        out_shape=[jax.ShapeDtypeStruct((N, SSM_CH), F32)] * 2,
        scratch_shapes=[pltpu.VMEM((R, S2), F32), pltpu.VMEM((R, S2), F32),
                        pltpu.VMEM((SUBLANE, S2), F32), pltpu.VMEM((SUBLANE, S2), F32)],
        compiler_params=_cparams(("arbitrary",)), name="ssm",
    )(u_rows, u_rows, lw["ssm_bm"], lw["ssm_coef" + str(nb)], lw["ssm_cm"])


def _mix_kernel(h_ref, oc_ref, u_ref, yf_ref, yb_ref, om_ref, d_ref, gw_ref, gb_ref, wo_ref, g_ref, b_ref,
                h1_ref, h1b_ref):
    y = d_ref[...] * u_ref[...] + yf_ref[...] + yb_ref[...]
    g = _gelu(y)
    o_ssm = g * _sigmoid(_dot(g.astype(BF16), gw_ref[...]) + gb_ref[...])
    mix = (_dot(oc_ref[...], wo_ref[0:256, :]) + _dot(o_ssm.astype(BF16), wo_ref[256:512, :])
           + _dot(om_ref[...], wo_ref[512:1024, :]))
    h1 = _layer_norm(DEEPNORM_ALPHA * h_ref[...] + mix, g_ref[...], b_ref[...])
    h1_ref[...] = h1
    h1b_ref[...] = h1.astype(BF16)


def _mix(h, o_conv, zs, yf, yb, o_mla, lw):
    B, L, D = h.shape
    TM = min(512, L)
    nt = L // TM
    row = lambda b, j: (b, j, 0)
    tb = lambda b, j: (j, b)
    const2 = lambda b, j: (0, 0)
    return pl.pallas_call(
        _mix_kernel,
        grid=(B, nt),
        in_specs=[
            pl.BlockSpec((None, TM, D), row),
            pl.BlockSpec((None, TM, CONV_CH), row),
            pl.BlockSpec((TM, SSM_CH), tb),
            pl.BlockSpec((TM, SSM_CH), tb),
            pl.BlockSpec((TM, SSM_CH), tb),
            pl.BlockSpec((None, TM, MLA_HEADS * MLA_V), row),
            pl.BlockSpec((1, SSM_CH), const2),
            pl.BlockSpec((SSM_CH, SSM_CH), const2),
            pl.BlockSpec((1, SSM_CH), const2),
            pl.BlockSpec((D, D), const2),
            pl.BlockSpec((1, D), const2),
            pl.BlockSpec((1, D), const2),
        ],
        out_specs=[pl.BlockSpec((None, TM, D), row), pl.BlockSpec((None, TM, D), row)],
        out_shape=[jax.ShapeDtypeStruct((B, L, D), F32), jax.ShapeDtypeStruct((B, L, D), BF16)],
        compiler_params=_cparams(("parallel", "parallel")), name="mix",
    )(h, o_conv, zs, yf, yb, o_mla, lw["ssm_d"], lw["ssm_glu_w"], lw["ssm_glu_b"], lw["w_out"],
      lw["ln1_g"], lw["ln1_b"])


def _top_desc(s, n):
    m = jnp.max(s, axis=0, keepdims=True)
    out = [m]
    for _ in range(n - 1):
        m = jnp.max(jnp.where(s < m, s, -jnp.inf), axis=0, keepdims=True)
        out.append(m)
    return out


def _split_bf16(x):
    hi = x.astype(BF16)
    return hi, (x - hi.astype(F32)).astype(BF16)


def _route_kernel(x_ref, wq_ref, k1_ref, k2_ref, ns1_ref, e1_ref, s2m_ref, e2_ref, cand_ref):
    q = _dot(x_ref[...], wq_ref[...])
    k_hl = [_split_bf16(k1_ref[...]), _split_bf16(k2_ref[...])]
    cand_ref[...] = jnp.full(cand_ref.shape, -jnp.inf, F32)
    for h in range(PEER_HEADS):
        s, tops = [], []
        for half in range(2):
            lo = h * 2 * PEER_HALF + half * PEER_HALF
            q_hi, q_lo = _split_bf16(q[:, lo:lo + PEER_HALF])
            k_hi, k_lo = k_hl[half]
            sc = _dot_nt(k_hi, q_hi) + (_dot_nt(k_hi, q_lo) + _dot_nt(k_lo, q_hi))
            s.append(sc)
            tops.append(_top_desc(sc, _CAND_N))
        for r, (a, b) in enumerate(_CAND_PAIRS):
            cand_ref[r:r + 1, :] = tops[0][a] + tops[1][b]
        cs = _top_desc(cand_ref[...], _CAND_N)
        thr = 0.5 * (cs[PEER_TOPK - 1] + cs[PEER_TOPK])
        z = jnp.ones_like(thr)
        for kk in range(1, PEER_TOPK):
            z = z + jnp.exp(cs[kk] - cs[0])
        ns1_ref[h] = -s[0]
        e1_ref[h] = jnp.exp(s[0] - tops[0][0]) / z
        s2m_ref[h] = s[1] - thr
        e2_ref[h] = jnp.exp(s[1] - tops[1][0])


def _route(xb, lw):
    T, D = xb.shape
    TT = min(256, T)
    aux = jax.ShapeDtypeStruct((PEER_HEADS, PEER_NKEYS, T), F32)
    aux_spec = pl.BlockSpec((PEER_HEADS, PEER_NKEYS, TT), lambda i: (0, 0, i))
    return pl.pallas_call(
        _route_kernel,
        grid=(T // TT,),
        in_specs=[
            pl.BlockSpec((TT, D), lambda i: (i, 0)),
            pl.BlockSpec((D, 2 * PEER_HEADS * PEER_HALF), lambda i: (0, 0)),
            pl.BlockSpec((PEER_NKEYS, PEER_HALF), lambda i: (0, 0)),
            pl.BlockSpec((PEER_NKEYS, PEER_HALF), lambda i: (0, 0)),
        ],
        out_specs=[aux_spec] * 4,
        out_shape=[aux] * 4,
        scratch_shapes=[pltpu.VMEM((_CAND_ROWS, TT), F32)],
        compiler_params=_cparams(("parallel",)), name="route",
    )(xb, lw["peer_w_q"], lw["peer_k1"], lw["peer_k2"])


def _peer_kernel(x_ref, u_ref, vt_ref, ns1_ref, e1_ref, s2m_ref, e2_ref, o_ref, act_ref, ag_ref, acc_ref,
                 *, ne, eb, tt):
    e = pl.program_id(1)

    @pl.when(e == 0)
    def _():
        acc_ref[...] = jnp.zeros(acc_ref.shape, F32)

    act_ref[...] = _dot_nt(u_ref[...], x_ref[...])
    n1 = eb // PEER_NKEYS
    rows = pl.ds(pl.multiple_of(e * n1, n1), n1)
    for c0 in range(0, tt, LANE):
        cols = pl.ds(c0, LANE)
        ns = [ns1_ref[h, rows, cols] for h in range(PEER_HEADS)]
        ee = [e1_ref[h, rows, cols] for h in range(PEER_HEADS)]
        for i in range(n1):
            g = jnp.zeros((PEER_NKEYS, LANE), F32)
            for h in range(PEER_HEADS):
                sel = s2m_ref[h, :, cols] > ns[h][i:i + 1, :]
                g = g + jnp.where(sel, e2_ref[h, :, cols] * ee[h][i:i + 1, :], 0.0)
            er = pl.ds(i * PEER_NKEYS, PEER_NKEYS)
            ag_ref[er, cols] = (_gelu(act_ref[er, cols]) * g).astype(BF16)
    acc_ref[...] += _dot(vt_ref[...], ag_ref[...])

    @pl.when(e == ne - 1)
    def _():
        o_ref[...] = acc_ref[...].T


def _peer(xb, aux, lw):
    T, D = xb.shape
    TT = min(512, T)
    EB = 1024
    ne = PEER_EXPERTS // EB
    aux_spec = pl.BlockSpec((PEER_HEADS, PEER_NKEYS, TT), lambda i, e: (0, 0, i))
    return pl.pallas_call(
        functools.partial(_peer_kernel, ne=ne, eb=EB, tt=TT),
        grid=(T // TT, ne),
        in_specs=[
            pl.BlockSpec((TT, D), lambda i, e: (i, 0)),
            pl.BlockSpec((EB, D), lambda i, e: (e, 0)),
            pl.BlockSpec((D, EB), lambda i, e: (0, e)),
        ] + [aux_spec] * 4,
        out_specs=pl.BlockSpec((TT, D), lambda i, e: (i, 0)),
        out_shape=jax.ShapeDtypeStruct((T, D), F32),
        scratch_shapes=[pltpu.VMEM((EB, TT), F32), pltpu.VMEM((EB, TT), BF16), pltpu.VMEM((D, TT), F32)],
        compiler_params=_cparams(("parallel", "arbitrary")), name="peer",
    )(xb, lw["peer_u"], lw["peer_vt"], *aux)


def _ple_kernel(h1_ref, h1b_ref, pe_ref, p_ref, wg_ref, wp_ref, g_ref, b_ref, o_ref):
    gate = _sigmoid(_dot(h1b_ref[...], wg_ref[...]))
    emb = _dot(p_ref[...].astype(BF16), wp_ref[...])
    r = DEEPNORM_ALPHA * h1_ref[...] + pe_ref[...] + gate * emb
    o_ref[...] = _layer_norm(r, g_ref[...], b_ref[...])


def _ple(h1, h1b, peer_out, p, lw):
    T, D = h1.shape
    TM = min(512, T)
    row = lambda i: (i, 0)
    const = lambda i: (0, 0)
    return pl.pallas_call(
        _ple_kernel,
        grid=(T // TM,),
        in_specs=[
            pl.BlockSpec((TM, D), row), pl.BlockSpec((TM, D), row), pl.BlockSpec((TM, D), row),
            pl.BlockSpec((TM, PLE_DIM), row),
            pl.BlockSpec((D, D), const), pl.BlockSpec((PLE_DIM, D), const),
            pl.BlockSpec((1, D), const), pl.BlockSpec((1, D), const),
        ],
        out_specs=pl.BlockSpec((TM, D), row),
        out_shape=jax.ShapeDtypeStruct((T, D), F32),
        compiler_params=_cparams(("parallel",)), name="ple",
    )(h1, h1b, peer_out, p, lw["ple_w_g"], lw["ple_w_p"], lw["ln2_g"], lw["ln2_b"])


def _ssm_params(a_re, a_im, b_re, b_im, c_re, c_im, log_dt):
    G, P, C = SSM_GROUPS, SSM_STATE, SSM_GROUP
    A = lax.complex(a_re.astype(F32), a_im.astype(F32))
    dt = jnp.exp(log_dt.astype(F32))[..., None]
    a_bar = jnp.exp(dt * A)
    b_bar = ((a_bar - 1.0) / A)[..., None] * lax.complex(b_re.astype(F32), b_im.astype(F32))
    eye = jnp.eye(G, dtype=F32)

    def bdiag_in(m):
        return jnp.einsum("dgpc,gh->dgchp", m, eye).reshape(2, G * C, G * P)

    def bdiag_out(m):
        return jnp.einsum("dgcp,gh->dgphc", m, eye).reshape(2, G * P, G * C)

    bm = jnp.concatenate([bdiag_in(jnp.real(b_bar)), bdiag_in(jnp.imag(b_bar))], axis=2).astype(BF16)
    cm = jnp.concatenate([bdiag_out(c_re.astype(F32)), -bdiag_out(c_im.astype(F32))], axis=1).astype(BF16)
    a1 = a_bar.reshape(2, G * P)
    a2 = a1 * a1
    zero = jnp.zeros_like(a1)

    def rows(top, bottom):
        shape = (top.shape[0], SUBLANE // 2, G * P)
        m = jnp.concatenate([jnp.broadcast_to(top[:, None, :], shape),
                             jnp.broadcast_to(bottom[:, None, :], shape)], axis=1)
        return jnp.real(m), jnp.imag(m)

    c8 = jnp.stack([*rows(a1, a1), *rows(zero, zero)], axis=1)
    f1, f2 = rows(a1[0:1], a2[0:1]), rows(zero[0:1], a1[0:1])
    b1, b2 = rows(a2[1:2], a1[1:2]), rows(a1[1:2], zero[1:2])
    c4 = jnp.concatenate([jnp.stack([*f1, *f2], axis=1), jnp.stack([*b1, *b2], axis=1)], axis=0)
    return bm, cm, c8.astype(F32), c4.astype(F32)


def _rope_tables(L):
    pos = jnp.arange(L, dtype=F32)
    inv_freq = ROPE_THETA ** (-jnp.arange(0, MLA_ROPE, 2, dtype=F32) / MLA_ROPE)
    ang = pos[:, None] * inv_freq[None, :]
    cos, sin = jnp.cos(ang), jnp.sin(ang)
    z = jnp.zeros_like(cos)
    return (jnp.concatenate([cos, cos, z, z], axis=1),
            jnp.concatenate([-sin, z, z, z], axis=1),
            jnp.concatenate([z, sin, z, z], axis=1))


def _layer_weights(i, w_in, conv_w, conv_b, conv_ln_g, conv_ln_b,
                   ssm_a_re, ssm_a_im, ssm_b_re, ssm_b_im, ssm_c_re, ssm_c_im, ssm_log_dt, ssm_d, ssm_glu_w,
                   ssm_glu_b, mla_q_norm_g, mla_w_uq, mla_kv_norm_g, mla_w_ukv, w_out, ln1_g, ln1_b,
                   peer_w_q, peer_k1, peer_k2, peer_u, peer_v, ple_w_p, ple_w_g, ln2_g, ln2_b):
    row = lambda v: v[i].reshape(1, -1).astype(F32)
    H = MLA_HEADS
    wuq = mla_w_uq[i].reshape(MLA_Q_RANK, H, MLA_DK)
    wuq = jnp.pad(wuq, ((0, 0), (0, 0), (0, MLA_DKP - MLA_DK))).reshape(MLA_Q_RANK, H * MLA_DKP)
    wukv = mla_w_ukv[i].reshape(MLA_KV_RANK, H, MLA_NOPE + MLA_V)
    wukv = jnp.concatenate([wukv[:, :, :MLA_NOPE].reshape(MLA_KV_RANK, H * MLA_NOPE),
                            wukv[:, :, MLA_NOPE:].reshape(MLA_KV_RANK, H * MLA_V)], axis=1)
    bm, cm, c8, c4 = _ssm_params(ssm_a_re[i], ssm_a_im[i], ssm_b_re[i], ssm_b_im[i], ssm_c_re[i], ssm_c_im[i],
                                 ssm_log_dt[i])
    return {
        "w_in": jnp.pad(w_in[i], ((0, 0), (0, IN_PAD - w_in.shape[2]))).astype(BF16),
        "q_norm_g": row(mla_q_norm_g), "w_uq": wuq.astype(BF16),
        "kv_norm_g": row(mla_kv_norm_g), "w_ukv": wukv.astype(BF16),
        "conv_w": conv_w[i].astype(F32), "conv_b": row(conv_b),
        "conv_ln_g": row(conv_ln_g), "conv_ln_b": row(conv_ln_b),
        "ssm_bm": bm, "ssm_cm": cm, "ssm_coef8": c8, "ssm_coef4": c4,
        "ssm_d": row(ssm_d), "ssm_glu_w": ssm_glu_w[i].astype(BF16), "ssm_glu_b": row(ssm_glu_b),
        "w_out": w_out[i].astype(BF16), "ln1_g": row(ln1_g), "ln1_b": row(ln1_b),
        "peer_w_q": peer_w_q[i].astype(BF16), "peer_k1": peer_k1[i].astype(F32), "peer_k2": peer_k2[i].astype(F32),
        "peer_u": peer_u[i].astype(BF16), "peer_vt": peer_v[i].T.astype(BF16),
        "ple_w_p": ple_w_p[i].astype(BF16), "ple_w_g": ple_w_g[i].astype(BF16),
        "ln2_g": row(ln2_g), "ln2_b": row(ln2_b),
    }


def _run(x, p, layers, ln_emb):
    B, L, D = x.shape
    assert B in (SUBLANE // 2, SUBLANE), "the S5 scan packs 4 or 8 batch rows per sublane group"
    rope_tabs = _rope_tables(L)
    h = x
    for i, lw in enumerate(layers):
        if i == 0:
            h, zc, zs, q, k, v = _in_proj(h, lw, rope_tabs, ln=ln_emb)
        else:
            zc, zs, q, k, v = _in_proj(h, lw, rope_tabs)
        o_mla = _attention(q, k, v)
        o_conv = _conv(zc, lw)
        yf, yb = _ssm(zs.reshape(L * B, SSM_CH), lw, B)
        h1, h1b = _mix(h, o_conv, zs, yf.reshape(L, B * SSM_CH), yb.reshape(L, B * SSM_CH), o_mla, lw)
        h1 = h1.reshape(B * L, D)
        h1b = h1b.reshape(B * L, D)
        aux = _route(h1b, lw)
        peer_out = _peer(h1b, aux, lw)
        h = _ple(h1, h1b, peer_out, p[i].reshape(B * L, PLE_DIM), lw).reshape(B, L, D)
    return h


def kernel(x_prompt, x_sample, p_prompt, p_sample, ln_emb_g, ln_emb_b, w_in, conv_w, conv_b, conv_ln_g, conv_ln_b, ssm_a_re, ssm_a_im, ssm_b_re, ssm_b_im, ssm_c_re, ssm_c_im, ssm_log_dt, ssm_d, ssm_glu_w, ssm_glu_b, mla_q_norm_g, mla_w_uq, mla_kv_norm_g, mla_w_ukv, w_out, ln1_g, ln1_b, peer_w_q, peer_k1, peer_k2, peer_u, peer_v, ple_w_p, ple_w_g, ln2_g, ln2_b):
    layers = [
        _layer_weights(i, w_in, conv_w, conv_b, conv_ln_g, conv_ln_b,
                       ssm_a_re, ssm_a_im, ssm_b_re, ssm_b_im, ssm_c_re, ssm_c_im, ssm_log_dt, ssm_d, ssm_glu_w,
                       ssm_glu_b, mla_q_norm_g, mla_w_uq, mla_kv_norm_g, mla_w_ukv, w_out, ln1_g, ln1_b,
                       peer_w_q, peer_k1, peer_k2, peer_u, peer_v, ple_w_p, ple_w_g, ln2_g, ln2_b)
        for i in range(DEPTH)
    ]
    ln_emb = (ln_emb_g.reshape(1, -1).astype(F32), ln_emb_b.reshape(1, -1).astype(F32))
    y_prompt = _run(x_prompt, p_prompt, layers, ln_emb)
    y_sample = _run(x_sample, p_sample, layers, ln_emb)
    return (y_prompt, y_sample)
```

```python
import functools
import math

import jax
import jax.numpy as jnp
import numpy as np
from jax import lax
from jax.experimental import pallas as pl
from jax.experimental.pallas import tpu as pltpu

F32 = jnp.float32
BF16 = jnp.bfloat16

D_MODEL = 1024
DEPTH = 2
CONV_CH = 256
CONV_WIDTH = 31
CONV_HALO = 16
SSM_CH = 256
SSM_GROUP = 16
SSM_GROUPS = 16
SSM_STATE = 64
SSM_LANES = SSM_GROUPS * SSM_STATE
MLA_HEADS = 4
MLA_NOPE = 128
MLA_ROPE = 64
MLA_V = 128
MLA_DK = MLA_NOPE + MLA_ROPE
MLA_DKP = 256
MLA_Q_RANK = 512
MLA_KV_RANK = 256
ROPE_THETA = 10000.0
PEER_HEADS = 8
PEER_NKEYS = 128
PEER_EXPERTS = PEER_NKEYS * PEER_NKEYS
PEER_HALF = 128
PEER_TOPK = 16
PLE_DIM = 256
DEEPNORM_ALPHA = float((2 * DEPTH) ** 0.25)
LN_EPS = 1e-5
RMS_EPS = 1e-6
IN_PAD = 1664

LANE = 128
SUBLANE = 8
VMEM_LIMIT = 56 * 1024 * 1024

_CAND_N = PEER_TOPK + 1
_CAND_PAIRS = tuple((i, j) for i in range(_CAND_N) for j in range(_CAND_N) if (i + 1) * (j + 1) <= _CAND_N)
_CAND_ROWS = -(-len(_CAND_PAIRS) // SUBLANE) * SUBLANE


def _cparams(sem, flags=None):
    return pltpu.CompilerParams(dimension_semantics=sem, vmem_limit_bytes=VMEM_LIMIT, flags=flags)


def _sigmoid(x):
    return 1.0 / (1.0 + jnp.exp(-x))


def _gelu(x):
    return 0.5 * x * (1.0 + jnp.tanh(math.sqrt(2.0 / math.pi) * (x + 0.044715 * (x * x * x))))


def _layer_norm(x, g, b):
    mu = jnp.mean(x, axis=-1, keepdims=True)
    xc = x - mu
    var = jnp.mean(xc * xc, axis=-1, keepdims=True)
    return xc * lax.rsqrt(var + LN_EPS) * g + b


def _rms_norm(x, g):
    return x * lax.rsqrt(jnp.mean(x * x, axis=-1, keepdims=True) + RMS_EPS) * g


def _dot(a, b):
    return jnp.dot(a, b, preferred_element_type=F32)


def _dot_nt(a, b):
    return lax.dot_general(a, b, (((1,), (1,)), ((), ())), preferred_element_type=F32)


def _in_proj_kernel(*refs, pre_ln):
    if pre_ln:
        (x_ref, g_ref, b_ref, w_ref, qg_ref, wuq_ref, kvg_ref, wukv_ref, rc_ref, rs1_ref, rs2_ref,
         h_ref, zc_ref, zs_ref, q_ref, k_ref, v_ref) = refs
    else:
        (x_ref, w_ref, qg_ref, wuq_ref, kvg_ref, wukv_ref, rc_ref, rs1_ref, rs2_ref,
         zc_ref, zs_ref, q_ref, k_ref, v_ref) = refs
    x = x_ref[...]
    if pre_ln:
        x = _layer_norm(x, g_ref[...], b_ref[...])
        h_ref[...] = x
    z = _dot(x.astype(BF16), w_ref[...])
    zc_ref[...] = z[:, 0:512]
    zs_ref[...] = z[:, 512:768]
    cq = _rms_norm(z[:, 768:1280], qg_ref[...])
    ckv = _rms_norm(z[:, 1280:1536], kvg_ref[...])
    kr = z[:, 1536:1664]
    q = _dot(cq.astype(BF16), wuq_ref[...])
    kv = _dot(ckv.astype(BF16), wukv_ref[...])
    rc, rs1, rs2 = rc_ref[...], rs1_ref[...], rs2_ref[...]

    def rope(t):
        return t * rc + pltpu.roll(t, 96, 1) * rs1 + pltpu.roll(t, 32, 1) * rs2

    scale = MLA_DK ** -0.5
    k_rope = rope(kr).astype(BF16)
    for h in range(MLA_HEADS):
        base = h * MLA_DKP
        q_ref[h, :, 0:128] = (q[:, base:base + 128] * scale).astype(BF16)
        q_ref[h, :, 128:256] = (rope(q[:, base + 128:base + 256]) * scale).astype(BF16)
        k_ref[h, :, 0:128] = kv[:, h * 128:(h + 1) * 128].astype(BF16)
        k_ref[h, :, 128:256] = k_rope
        v_ref[h] = kv[:, 512 + h * 128:512 + (h + 1) * 128].astype(BF16)


def _in_proj(x, lw, rope_tabs, ln=None):
    B, L, D = x.shape
    TM = min(512, L)
    nt = L // TM
    pre_ln = ln is not None
    row = lambda b, j: (b, j, 0)
    const2 = lambda b, j: (0, 0)
    in_specs = [pl.BlockSpec((None, TM, D), row)]
    args = [x]
    if pre_ln:
        in_specs += [pl.BlockSpec((1, D), const2)] * 2
        args += [ln[0], ln[1]]
    in_specs += [
        pl.BlockSpec((D, IN_PAD), const2),
        pl.BlockSpec((1, MLA_Q_RANK), const2),
        pl.BlockSpec((MLA_Q_RANK, MLA_HEADS * MLA_DKP), const2),
        pl.BlockSpec((1, MLA_KV_RANK), const2),
        pl.BlockSpec((MLA_KV_RANK, 2 * MLA_HEADS * 128), const2),
    ] + [pl.BlockSpec((TM, LANE), lambda b, j: (j, 0))] * 3
    args += [lw["w_in"], lw["q_norm_g"], lw["w_uq"], lw["kv_norm_g"], lw["w_ukv"], *rope_tabs]
    out_shape, out_specs = [], []
    if pre_ln:
        out_shape.append(jax.ShapeDtypeStruct((B, L, D), F32))
        out_specs.append(pl.BlockSpec((None, TM, D), row))
    out_shape += [
        jax.ShapeDtypeStruct((B, L, 2 * CONV_CH), F32),
        jax.ShapeDtypeStruct((L, B * SSM_CH), F32),
        jax.ShapeDtypeStruct((B, MLA_HEADS, L, MLA_DKP), BF16),
        jax.ShapeDtypeStruct((B, MLA_HEADS, L, MLA_DKP), BF16),
        jax.ShapeDtypeStruct((B, MLA_HEADS, L, MLA_V), BF16),
    ]
    hrow = lambda b, j: (b, 0, j, 0)
    out_specs += [
        pl.BlockSpec((None, TM, 2 * CONV_CH), row),
        pl.BlockSpec((TM, SSM_CH), lambda b, j: (j, b)),
        pl.BlockSpec((None, MLA_HEADS, TM, MLA_DKP), hrow),
        pl.BlockSpec((None, MLA_HEADS, TM, MLA_DKP), hrow),
        pl.BlockSpec((None, MLA_HEADS, TM, MLA_V), hrow),
    ]
    return pl.pallas_call(
        functools.partial(_in_proj_kernel, pre_ln=pre_ln),
        grid=(B, nt), in_specs=in_specs, out_specs=out_specs, out_shape=out_shape,
        compiler_params=_cparams(("parallel", "parallel")), name="in_proj",
    )(*args)


def _attn_kernel(q_ref, k_ref, v_ref, o_ref, m_ref, l_ref, acc_ref, *, nk):
    j = pl.program_id(3)

    @pl.when(j == 0)
    def _():
        m_ref[...] = jnp.full(m_ref.shape, -jnp.inf, F32)
        l_ref[...] = jnp.zeros(l_ref.shape, F32)
        acc_ref[...] = jnp.zeros(acc_ref.shape, F32)

    s = _dot_nt(q_ref[...], k_ref[...])
    m_prev = m_ref[...]
    m_new = jnp.maximum(m_prev, jnp.max(s, axis=1, keepdims=True))
    alpha = jnp.exp(m_prev - m_new)
    p = jnp.exp(s - m_new)
    l_ref[...] = alpha * l_ref[...] + jnp.sum(p, axis=1, keepdims=True)
    acc_ref[...] = alpha * acc_ref[...] + _dot(p.astype(BF16), v_ref[...])
    m_ref[...] = m_new

    @pl.when(j == nk - 1)
    def _():
        o_ref[...] = (acc_ref[...] / l_ref[...]).astype(o_ref.dtype)


def _attention(q, k, v):
    B, H, L, _ = q.shape
    TQ = min(1024, L)
    TK = min(1024, L)
    nq, nk = L // TQ, L // TK
    return pl.pallas_call(
        functools.partial(_attn_kernel, nk=nk),
        grid=(B, H, nq, nk),
        in_specs=[
            pl.BlockSpec((None, None, TQ, MLA_DKP), lambda b, h, i, j: (b, h, i, 0)),
            pl.BlockSpec((None, None, TK, MLA_DKP), lambda b, h, i, j: (b, h, j, 0)),
            pl.BlockSpec((None, None, TK, MLA_V), lambda b, h, i, j: (b, h, j, 0)),
        ],
        out_specs=pl.BlockSpec((None, TQ, MLA_V), lambda b, h, i, j: (b, i, h)),
        out_shape=jax.ShapeDtypeStruct((B, L, H * MLA_V), BF16),
        scratch_shapes=[pltpu.VMEM((TQ, 1), F32), pltpu.VMEM((TQ, 1), F32), pltpu.VMEM((TQ, MLA_V), F32)],
        compiler_params=_cparams(("parallel", "parallel", "parallel", "arbitrary")), name="attn",
    )(q, k, v)


def _conv_kernel(zm_ref, zp_ref, zn_ref, w_ref, cb_ref, g_ref, b_ref, o_ref, hp_ref, *, tc, nt, rc):
    j = pl.program_id(1)

    def glu(z):
        return z[:, :CONV_CH] * _sigmoid(z[:, CONV_CH:])

    hp_ref[0:CONV_HALO, :] = jnp.where(j > 0, glu(zp_ref[...]), 0.0)
    hp_ref[CONV_HALO:CONV_HALO + tc, :] = glu(zm_ref[...])
    hp_ref[CONV_HALO + tc:2 * CONV_HALO + tc, :] = jnp.where(j < nt - 1, glu(zn_ref[...]), 0.0)
    shift = CONV_HALO - CONV_WIDTH // 2
    for r in range(tc // rc):
        acc = jnp.zeros((rc, CONV_CH), F32)
        for kk in range(CONV_WIDTH):
            lo = r * rc + kk + shift
            acc = acc + hp_ref[lo:lo + rc, :] * w_ref[kk:kk + 1, :]
        y = _layer_norm(acc + cb_ref[...], g_ref[...], b_ref[...])
        o_ref[r * rc:(r + 1) * rc, :] = (y * _sigmoid(y)).astype(o_ref.dtype)


def _conv(zc, lw):
    B, L, _ = zc.shape
    TC = min(512, L)
    nt = L // TC
    hb = TC // CONV_HALO
    nhb = L // CONV_HALO
    const2 = lambda b, j: (0, 0)
    return pl.pallas_call(
        functools.partial(_conv_kernel, tc=TC, nt=nt, rc=min(64, TC)),
        grid=(B, nt),
        in_specs=[
            pl.BlockSpec((None, TC, 2 * CONV_CH), lambda b, j: (b, j, 0)),
            pl.BlockSpec((None, CONV_HALO, 2 * CONV_CH), lambda b, j: (b, jnp.maximum(j * hb - 1, 0), 0)),
            pl.BlockSpec((None, CONV_HALO, 2 * CONV_CH), lambda b, j: (b, jnp.minimum((j + 1) * hb, nhb - 1), 0)),
            pl.BlockSpec((CONV_WIDTH, CONV_CH), const2),
            pl.BlockSpec((1, CONV_CH), const2),
            pl.BlockSpec((1, CONV_CH), const2),
            pl.BlockSpec((1, CONV_CH), const2),
        ],
        out_specs=pl.BlockSpec((None, TC, CONV_CH), lambda b, j: (b, j, 0)),
        out_shape=jax.ShapeDtypeStruct((B, L, CONV_CH), BF16),
        scratch_shapes=[pltpu.VMEM((TC + 2 * CONV_HALO, CONV_CH), F32)],
        compiler_params=_cparams(("parallel", "parallel")), name="conv",
    )(zc, zc, zc, lw["conv_w"], lw["conv_b"], lw["conv_ln_g"], lw["conv_ln_b"])


def _ssm_kernel(uf_ref, ub_ref, bm_ref, coef_ref, cm_ref, yf_ref, yb_ref,
                sf_ref, sb_ref, hf_ref, hb_ref, *, nb, steps, lc):
    i = pl.program_id(0)
    S = SSM_LANES

    @pl.when(i == 0)
    def _():
        hf_ref[...] = jnp.zeros(hf_ref.shape, F32)
        hb_ref[...] = jnp.zeros(hb_ref.shape, F32)

    sf_ref[...] = _dot(uf_ref[...].astype(BF16), bm_ref[0])
    sb_ref[...] = _dot(ub_ref[...].astype(BF16), bm_ref[1])

    lo_half = lax.broadcasted_iota(jnp.int32, (SUBLANE, lc), 0) < nb

    for c0 in range(0, S, lc):
        re_l = pl.ds(c0, lc)
        im_l = pl.ds(S + c0, lc)
        cf = [coef_ref[0, k, :, c0:c0 + lc] for k in range(4)]
        cb = [coef_ref[1, k, :, c0:c0 + lc] for k in range(4)]

        def body(t, carry, re_l=re_l, im_l=im_l, cf=cf, cb=cb):
            hfr, hfi, hbr, hbi = carry
            rf = pl.ds(pl.multiple_of(t * SUBLANE, SUBLANE), SUBLANE)
            rb = pl.ds(pl.multiple_of((steps - 1 - t) * SUBLANE, SUBLANE), SUBLANE)
            out = []
            for ref, rows, c, hr, hi, fwd in ((sf_ref, rf, cf, hfr, hfi, True), (sb_ref, rb, cb, hbr, hbi, False)):
                xr = ref[rows, re_l]
                xi = ref[rows, im_l]
                nr = c[0] * hr - c[1] * hi + xr
                ni = c[0] * hi + c[1] * hr + xi
                if nb != SUBLANE:
                    sxr = pltpu.roll(xr, nb, 0)
                    sxi = pltpu.roll(xi, nb, 0)
                    nr = nr + (c[2] * sxr - c[3] * sxi)
                    ni = ni + (c[2] * sxi + c[3] * sxr)
                ref[rows, re_l] = nr
                ref[rows, im_l] = ni
                if nb != SUBLANE:
                    snr = pltpu.roll(nr, nb, 0)
                    sni = pltpu.roll(ni, nb, 0)
                    if fwd:
                        nr = jnp.where(lo_half, snr, nr)
                        ni = jnp.where(lo_half, sni, ni)
                    else:
                        nr = jnp.where(lo_half, nr, snr)
                        ni = jnp.where(lo_half, ni, sni)
                out += [nr, ni]
            return tuple(out)

        init = (hf_ref[:, re_l], hf_ref[:, im_l], hb_ref[:, re_l], hb_ref[:, im_l])
        hfr, hfi, hbr, hbi = lax.fori_loop(0, steps, body, init, unroll=2)
        hf_ref[:, re_l] = hfr
        hf_ref[:, im_l] = hfi
        hb_ref[:, re_l] = hbr
        hb_ref[:, im_l] = hbi

    yf_ref[...] = _dot(sf_ref[...].astype(BF16), cm_ref[0])
    yb_ref[...] = _dot(sb_ref[...].astype(BF16), cm_ref[1])


def _ssm(u_rows, lw, nb):
    N = u_rows.shape[0]
    R = min(1024, N)
    n = N // R
    steps = R // SUBLANE
    S2 = 2 * SSM_LANES
    return pl.pallas_call(
        functools.partial(_ssm_kernel, nb=nb, steps=steps, lc=512),
        grid=(n,),
        in_specs=[
            pl.BlockSpec((R, SSM_CH), lambda i: (i, 0)),
            pl.BlockSpec((R, SSM_CH), lambda i: (n - 1 - i, 0)),
            pl.BlockSpec((2, SSM_CH, S2), lambda i: (0, 0, 0)),
            pl.BlockSpec((2, 4, SUBLANE, SSM_LANES), lambda i: (0, 0, 0, 0)),
            pl.BlockSpec((2, S2, SSM_CH), lambda i: (0, 0, 0)),
        ],
        out_specs=[
            pl.BlockSpec((R, SSM_CH), lambda i: (i, 0)),
            pl.BlockSpec((R, SSM_CH), lambda i: (n - 1 - i, 0)),
        ],
        out_shape=[jax.ShapeDtypeStruct((N, SSM_CH), F32)] * 2,
        scratch_shapes=[pltpu.VMEM((R, S2), F32), pltpu.VMEM((R, S2), F32),
                        pltpu.VMEM((SUBLANE, S2), F32), pltpu.VMEM((SUBLANE, S2), F32)],
        compiler_params=_cparams(("arbitrary",)), name="ssm",
    )(u_rows, u_rows, lw["ssm_bm"], lw["ssm_coef" + str(nb)], lw["ssm_cm"])


def _mix_kernel(h_ref, oc_ref, u_ref, yf_ref, yb_ref, om_ref, d_ref, gw_ref, gb_ref, wo_ref, g_ref, b_ref,
                h1_ref, h1b_ref, h1t_ref):
    y = d_ref[...] * u_ref[...] + yf_ref[...] + yb_ref[...]
    g = _gelu(y)
    o_ssm = g * _sigmoid(_dot(g.astype(BF16), gw_ref[...]) + gb_ref[...])
    mix = (_dot(oc_ref[...], wo_ref[0:256, :]) + _dot(o_ssm.astype(BF16), wo_ref[256:512, :])
           + _dot(om_ref[...], wo_ref[512:1024, :]))
    h1 = _layer_norm(DEEPNORM_ALPHA * h_ref[...] + mix, g_ref[...], b_ref[...])
    h1_ref[...] = h1
    h1b_ref[...] = h1.astype(BF16)
    h1t_ref[...] = h1.T.astype(BF16)


def _mix(h, o_conv, zs, yf, yb, o_mla, lw):
    B, L, D = h.shape
    TM = min(512, L)
    nt = L // TM
    row = lambda b, j: (b, j, 0)
    tb = lambda b, j: (j, b)
    const2 = lambda b, j: (0, 0)
    return pl.pallas_call(
        _mix_kernel,
        grid=(B, nt),
        in_specs=[
            pl.BlockSpec((None, TM, D), row),
            pl.BlockSpec((None, TM, CONV_CH), row),
            pl.BlockSpec((TM, SSM_CH), tb),
            pl.BlockSpec((TM, SSM_CH), tb),
            pl.BlockSpec((TM, SSM_CH), tb),
            pl.BlockSpec((None, TM, MLA_HEADS * MLA_V), row),
            pl.BlockSpec((1, SSM_CH), const2),
            pl.BlockSpec((SSM_CH, SSM_CH), const2),
            pl.BlockSpec((1, SSM_CH), const2),
            pl.BlockSpec((D, D), const2),
            pl.BlockSpec((1, D), const2),
            pl.BlockSpec((1, D), const2),
        ],
        out_specs=[pl.BlockSpec((None, TM, D), row), pl.BlockSpec((None, TM, D), row),
                   pl.BlockSpec((D, TM), lambda b, j: (0, b * nt + j))],
        out_shape=[jax.ShapeDtypeStruct((B, L, D), F32), jax.ShapeDtypeStruct((B, L, D), BF16),
                   jax.ShapeDtypeStruct((D, B * L), BF16)],
        compiler_params=_cparams(("parallel", "parallel")), name="mix",
    )(h, o_conv, zs, yf, yb, o_mla, lw["ssm_d"], lw["ssm_glu_w"], lw["ssm_glu_b"], lw["w_out"],
      lw["ln1_g"], lw["ln1_b"])


def _top_desc(s, n):
    m = jnp.max(s, axis=0, keepdims=True)
    out = [m]
    for _ in range(n - 1):
        m = jnp.max(jnp.where(s < m, s, -jnp.inf), axis=0, keepdims=True)
        out.append(m)
    return out


def _split_bf16(x):
    hi = x.astype(BF16)
    return hi, (x - hi.astype(F32)).astype(BF16)


def _route_kernel(x_ref, wq_ref, k1_ref, k2_ref, ns1_ref, e1_ref, s2m_ref, e2_ref, cand_ref):
    q = _dot(x_ref[...], wq_ref[...])
    k_hl = [_split_bf16(k1_ref[...]), _split_bf16(k2_ref[...])]
    cand_ref[...] = jnp.full(cand_ref.shape, -jnp.inf, F32)
    for h in range(PEER_HEADS):
        s, tops = [], []
        for half in range(2):
            lo = h * 2 * PEER_HALF + half * PEER_HALF
            q_hi, q_lo = _split_bf16(q[:, lo:lo + PEER_HALF])
            k_hi, k_lo = k_hl[half]
            sc = _dot_nt(k_hi, q_hi) + (_dot_nt(k_hi, q_lo) + _dot_nt(k_lo, q_hi))
            s.append(sc)
            tops.append(_top_desc(sc, _CAND_N))
        for r, (a, b) in enumerate(_CAND_PAIRS):
            cand_ref[r:r + 1, :] = tops[0][a] + tops[1][b]
        cs = _top_desc(cand_ref[...], _CAND_N)
        thr = 0.5 * (cs[PEER_TOPK - 1] + cs[PEER_TOPK])
        z = jnp.ones_like(thr)
        for kk in range(1, PEER_TOPK):
            z = z + jnp.exp(cs[kk] - cs[0])
        outs = ((ns1_ref, -s[0]), (e1_ref, jnp.exp(s[0] - tops[0][0]) / z),
                (s2m_ref, s[1] - thr), (e2_ref, jnp.exp(s[1] - tops[1][0])))
        for ref, val in outs:
            for c in range(val.shape[1] // LANE):
                ref[h, c] = val[:, c * LANE:(c + 1) * LANE]


def _route(xb, lw):
    T, D = xb.shape
    TT = min(256, T)
    aux = jax.ShapeDtypeStruct((PEER_HEADS, T // LANE, PEER_NKEYS, LANE), F32)
    aux_spec = pl.BlockSpec((PEER_HEADS, TT // LANE, PEER_NKEYS, LANE), lambda i: (0, i, 0, 0))
    return pl.pallas_call(
        _route_kernel,
        grid=(T // TT,),
        in_specs=[
            pl.BlockSpec((TT, D), lambda i: (i, 0)),
            pl.BlockSpec((D, 2 * PEER_HEADS * PEER_HALF), lambda i: (0, 0)),
            pl.BlockSpec((PEER_NKEYS, PEER_HALF), lambda i: (0, 0)),
            pl.BlockSpec((PEER_NKEYS, PEER_HALF), lambda i: (0, 0)),
        ],
        out_specs=[aux_spec] * 4,
        out_shape=[aux] * 4,
        scratch_shapes=[pltpu.VMEM((_CAND_ROWS, TT), F32)],
        compiler_params=_cparams(("parallel",)), name="route",
    )(xb, lw["peer_w_q"], lw["peer_k1"], lw["peer_k2"])


MXU_TILE = 256


def _peer_kernel(xt_ref, u_ref, vt_ref, ns1_ref, e1_ref, s2m_ref, e2_ref, o_ref, act_ref, ag_ref, acc_ref,
                 *, ne, eb, tt):
    e = pl.program_id(1)

    @pl.when(e == 0)
    def _():
        acc_ref[...] = jnp.zeros(acc_ref.shape, F32)

    n1 = eb // PEER_NKEYS
    rows = pl.ds(pl.multiple_of(e * n1, n1), n1)
    per_tile = MXU_TILE // LANE
    for tn in range(tt // MXU_TILE):
        tcols = pl.ds(tn * MXU_TILE, MXU_TILE)
        for tk in range(eb // MXU_TILE):
            trows = pl.ds(tk * MXU_TILE, MXU_TILE)
            act_ref[trows, tcols] = _dot(u_ref[trows, :], xt_ref[:, tcols])
            for c in range(tn * per_tile, (tn + 1) * per_tile):
                cols = pl.ds(c * LANE, LANE)
                ns = [ns1_ref[h, c, rows, :] for h in range(PEER_HEADS)]
                ee = [e1_ref[h, c, rows, :] for h in range(PEER_HEADS)]
                for i in range(tk * per_tile, (tk + 1) * per_tile):
                    gate = jnp.zeros((PEER_NKEYS, LANE), F32)
                    for h in range(PEER_HEADS):
                        sel = s2m_ref[h, c] > ns[h][i:i + 1, :]
                        gate = gate + jnp.where(sel, e2_ref[h, c] * ee[h][i:i + 1, :], 0.0)
                    er = pl.ds(i * PEER_NKEYS, PEER_NKEYS)
                    ag_ref[er, cols] = (_gelu(act_ref[er, cols]) * gate).astype(BF16)
    acc_ref[...] += _dot(vt_ref[...], ag_ref[...])

    @pl.when(e == ne - 1)
    def _():
        o_ref[...] = acc_ref[...].T


def _peer(xt, aux, lw):
    D, T = xt.shape
    TT = min(512, T)
    EB = 1024
    ne = PEER_EXPERTS // EB
    aux_spec = pl.BlockSpec((PEER_HEADS, TT // LANE, PEER_NKEYS, LANE), lambda i, e: (0, i, 0, 0))
    return pl.pallas_call(
        functools.partial(_peer_kernel, ne=ne, eb=EB, tt=TT),
        grid=(T // TT, ne),
        in_specs=[
            pl.BlockSpec((D, TT), lambda i, e: (0, i)),
            pl.BlockSpec((EB, D), lambda i, e: (e, 0)),
            pl.BlockSpec((D, EB), lambda i, e: (0, e)),
        ] + [aux_spec] * 4,
        out_specs=pl.BlockSpec((TT, D), lambda i, e: (i, 0)),
        out_shape=jax.ShapeDtypeStruct((T, D), F32),
        scratch_shapes=[pltpu.VMEM((EB, TT), F32), pltpu.VMEM((EB, TT), BF16), pltpu.VMEM((D, TT), F32)],
        compiler_params=_cparams(("parallel", "arbitrary")), name="peer",
    )(xt, lw["peer_u"], lw["peer_vt"], *aux)


def _ple_kernel(h1_ref, h1b_ref, pe_ref, p_ref, wg_ref, wp_ref, g_ref, b_ref, o_ref):
    gate = _sigmoid(_dot(h1b_ref[...], wg_ref[...]))
    emb = _dot(p_ref[...].astype(BF16), wp_ref[...])
    r = DEEPNORM_ALPHA * h1_ref[...] + pe_ref[...] + gate * emb
    o_ref[...] = _layer_norm(r, g_ref[...], b_ref[...])


def _ple(h1, h1b, peer_out, p, lw):
    T, D = h1.shape
    TM = min(512, T)
    row = lambda i: (i, 0)
    const = lambda i: (0, 0)
    return pl.pallas_call(
        _ple_kernel,
        grid=(T // TM,),
        in_specs=[
            pl.BlockSpec((TM, D), row), pl.BlockSpec((TM, D), row), pl.BlockSpec((TM, D), row),
            pl.BlockSpec((TM, PLE_DIM), row),
            pl.BlockSpec((D, D), const), pl.BlockSpec((PLE_DIM, D), const),
            pl.BlockSpec((1, D), const), pl.BlockSpec((1, D), const),
        ],
        out_specs=pl.BlockSpec((TM, D), row),
        out_shape=jax.ShapeDtypeStruct((T, D), F32),
        compiler_params=_cparams(("parallel",)), name="ple",
    )(h1, h1b, peer_out, p, lw["ple_w_g"], lw["ple_w_p"], lw["ln2_g"], lw["ln2_b"])


def _ssm_params(a_re, a_im, b_re, b_im, c_re, c_im, log_dt):
    G, P, C = SSM_GROUPS, SSM_STATE, SSM_GROUP
    ar, ai = a_re.astype(F32), a_im.astype(F32)
    dt = jnp.exp(log_dt.astype(F32))[..., None]
    mag = jnp.exp(dt * ar)
    abr, abi = mag * jnp.cos(dt * ai), mag * jnp.sin(dt * ai)
    den = ar * ar + ai * ai
    qr = ((abr - 1.0) * ar + abi * ai) / den
    qi = (abi * ar - (abr - 1.0) * ai) / den
    br, bi = b_re.astype(F32), b_im.astype(F32)
    bbr = qr[..., None] * br - qi[..., None] * bi
    bbi = qr[..., None] * bi + qi[..., None] * br
    eye = jnp.eye(G, dtype=F32)

    def bdiag_in(m):
        return jnp.einsum("dgpc,gh->dgchp", m, eye).reshape(2, G * C, G * P)

    def bdiag_out(m):
        return jnp.einsum("dgcp,gh->dgphc", m, eye).reshape(2, G * P, G * C)

    bm = jnp.concatenate([bdiag_in(bbr), bdiag_in(bbi)], axis=2).astype(BF16)
    cm = jnp.concatenate([bdiag_out(c_re.astype(F32)), -bdiag_out(c_im.astype(F32))], axis=1).astype(BF16)
    a1 = (abr.reshape(2, G * P), abi.reshape(2, G * P))
    a2 = (a1[0] * a1[0] - a1[1] * a1[1], 2.0 * a1[0] * a1[1])
    zero = (jnp.zeros_like(a1[0]), jnp.zeros_like(a1[0]))
    pick = lambda z, d: (z[0][d:d + 1], z[1][d:d + 1])

    def rows(top, bottom):
        shape = (top[0].shape[0], SUBLANE // 2, G * P)
        return [jnp.concatenate([jnp.broadcast_to(t[:, None, :], shape),
                                 jnp.broadcast_to(b[:, None, :], shape)], axis=1) for t, b in zip(top, bottom)]

    c8 = jnp.stack([*rows(a1, a1), *rows(zero, zero)], axis=1)
    f1, f2 = rows(pick(a1, 0), pick(a2, 0)), rows(pick(zero, 0), pick(a1, 0))
    b1, b2 = rows(pick(a2, 1), pick(a1, 1)), rows(pick(a1, 1), pick(zero, 1))
    c4 = jnp.concatenate([jnp.stack([*f1, *f2], axis=1), jnp.stack([*b1, *b2], axis=1)], axis=0)
    return bm, cm, c8.astype(F32), c4.astype(F32)


def _rope_tables(L):
    pos = jnp.arange(L, dtype=F32)
    inv_freq = ROPE_THETA ** (-jnp.arange(0, MLA_ROPE, 2, dtype=F32) / MLA_ROPE)
    ang = pos[:, None] * inv_freq[None, :]
    cos, sin = jnp.cos(ang), jnp.sin(ang)
    z = jnp.zeros_like(cos)
    return (jnp.concatenate([cos, cos, z, z], axis=1),
            jnp.concatenate([-sin, z, z, z], axis=1),
            jnp.concatenate([z, sin, z, z], axis=1))


def _layer_weights(i, w_in, conv_w, conv_b, conv_ln_g, conv_ln_b,
                   ssm_a_re, ssm_a_im, ssm_b_re, ssm_b_im, ssm_c_re, ssm_c_im, ssm_log_dt, ssm_d, ssm_glu_w,
                   ssm_glu_b, mla_q_norm_g, mla_w_uq, mla_kv_norm_g, mla_w_ukv, w_out, ln1_g, ln1_b,
                   peer_w_q, peer_k1, peer_k2, peer_u, peer_v, ple_w_p, ple_w_g, ln2_g, ln2_b):
    row = lambda v: v[i].reshape(1, -1).astype(F32)
    H = MLA_HEADS
    wuq = mla_w_uq[i].reshape(MLA_Q_RANK, H, MLA_DK)
    wuq = jnp.pad(wuq, ((0, 0), (0, 0), (0, MLA_DKP - MLA_DK))).reshape(MLA_Q_RANK, H * MLA_DKP)
    wukv = mla_w_ukv[i].reshape(MLA_KV_RANK, H, MLA_NOPE + MLA_V)
    wukv = jnp.concatenate([wukv[:, :, :MLA_NOPE].reshape(MLA_KV_RANK, H * MLA_NOPE),
                            wukv[:, :, MLA_NOPE:].reshape(MLA_KV_RANK, H * MLA_V)], axis=1)
    bm, cm, c8, c4 = _ssm_params(ssm_a_re[i], ssm_a_im[i], ssm_b_re[i], ssm_b_im[i], ssm_c_re[i], ssm_c_im[i],
                                 ssm_log_dt[i])
    return {
        "w_in": jnp.pad(w_in[i], ((0, 0), (0, IN_PAD - w_in.shape[2]))).astype(BF16),
        "q_norm_g": row(mla_q_norm_g), "w_uq": wuq.astype(BF16),
        "kv_norm_g": row(mla_kv_norm_g), "w_ukv": wukv.astype(BF16),
        "conv_w": conv_w[i].astype(F32), "conv_b": row(conv_b),
        "conv_ln_g": row(conv_ln_g), "conv_ln_b": row(conv_ln_b),
        "ssm_bm": bm, "ssm_cm": cm, "ssm_coef8": c8, "ssm_coef4": c4,
        "ssm_d": row(ssm_d), "ssm_glu_w": ssm_glu_w[i].astype(BF16), "ssm_glu_b": row(ssm_glu_b),
        "w_out": w_out[i].astype(BF16), "ln1_g": row(ln1_g), "ln1_b": row(ln1_b),
        "peer_w_q": peer_w_q[i].astype(BF16), "peer_k1": peer_k1[i].astype(F32), "peer_k2": peer_k2[i].astype(F32),
        "peer_u": peer_u[i].astype(BF16), "peer_vt": peer_v[i].T.astype(BF16),
        "ple_w_p": ple_w_p[i].astype(BF16), "ple_w_g": ple_w_g[i].astype(BF16),
        "ln2_g": row(ln2_g), "ln2_b": row(ln2_b),
    }


def _run(x, p, layers, ln_emb):
    B, L, D = x.shape
    assert B in (SUBLANE // 2, SUBLANE), "the S5 scan packs 4 or 8 batch rows per sublane group"
    rope_tabs = _rope_tables(L)
    h = x
    for i, lw in enumerate(layers):
        if i == 0:
            h, zc, zs, q, k, v = _in_proj(h, lw, rope_tabs, ln=ln_emb)
        else:
            zc, zs, q, k, v = _in_proj(h, lw, rope_tabs)
        o_mla = _attention(q, k, v)
        o_conv = _conv(zc, lw)
        yf, yb = _ssm(zs.reshape(L * B, SSM_CH), lw, B)
        h1, h1b, h1t = _mix(h, o_conv, zs, yf.reshape(L, B * SSM_CH), yb.reshape(L, B * SSM_CH), o_mla, lw)
        h1 = h1.reshape(B * L, D)
        h1b = h1b.reshape(B * L, D)
        aux = _route(h1b, lw)
        peer_out = _peer(h1t, aux, lw)
        h = _ple(h1, h1b, peer_out, p[i].reshape(B * L, PLE_DIM), lw).reshape(B, L, D)
    return h


def kernel(x_prompt, x_sample, p_prompt, p_sample, ln_emb_g, ln_emb_b, w_in, conv_w, conv_b, conv_ln_g, conv_ln_b, ssm_a_re, ssm_a_im, ssm_b_re, ssm_b_im, ssm_c_re, ssm_c_im, ssm_log_dt, ssm_d, ssm_glu_w, ssm_glu_b, mla_q_norm_g, mla_w_uq, mla_kv_norm_g, mla_w_ukv, w_out, ln1_g, ln1_b, peer_w_q, peer_k1, peer_k2, peer_u, peer_v, ple_w_p, ple_w_g, ln2_g, ln2_b):
    layers = [
        _layer_weights(i, w_in, conv_w, conv_b, conv_ln_g, conv_ln_b,
                       ssm_a_re, ssm_a_im, ssm_b_re, ssm_b_im, ssm_c_re, ssm_c_im, ssm_log_dt, ssm_d, ssm_glu_w,
                       ssm_glu_b, mla_q_norm_g, mla_w_uq, mla_kv_norm_g, mla_w_ukv, w_out, ln1_g, ln1_b,
                       peer_w_q, peer_k1, peer_k2, peer_u, peer_v, ple_w_p, ple_w_g, ln2_g, ln2_b)
        for i in range(DEPTH)
    ]
    ln_emb = (ln_emb_g.reshape(1, -1).astype(F32), ln_emb_b.reshape(1, -1).astype(F32))
    y_prompt = _run(x_prompt, p_prompt, layers, ln_emb)
    y_sample = _run(x_sample, p_sample, layers, ln_emb)
    return (y_prompt, y_sample)
```

```python
import functools
import math

import jax
import jax.numpy as jnp
import numpy as np
from jax import lax
from jax.experimental import pallas as pl
from jax.experimental.pallas import tpu as pltpu

F32 = jnp.float32
BF16 = jnp.bfloat16

D_MODEL = 1024
DEPTH = 2
CONV_CH = 256
CONV_WIDTH = 31
CONV_HALO = 16
SSM_CH = 256
SSM_GROUP = 16
SSM_GROUPS = 16
SSM_STATE = 64
SSM_LANES = SSM_GROUPS * SSM_STATE
MLA_HEADS = 4
MLA_NOPE = 128
MLA_ROPE = 64
MLA_V = 128
MLA_DK = MLA_NOPE + MLA_ROPE
MLA_DKP = 256
MLA_Q_RANK = 512
MLA_KV_RANK = 256
ROPE_THETA = 10000.0
PEER_HEADS = 8
PEER_NKEYS = 128
PEER_EXPERTS = PEER_NKEYS * PEER_NKEYS
PEER_HALF = 128
PEER_TOPK = 16
PLE_DIM = 256
DEEPNORM_ALPHA = float((2 * DEPTH) ** 0.25)
LN_EPS = 1e-5
RMS_EPS = 1e-6
IN_PAD = 1664

LANE = 128
SUBLANE = 8
VMEM_LIMIT = 56 * 1024 * 1024

_CAND_N = PEER_TOPK + 1
_CAND_PAIRS = tuple((i, j) for i in range(_CAND_N) for j in range(_CAND_N) if (i + 1) * (j + 1) <= _CAND_N)
_CAND_ROWS = -(-len(_CAND_PAIRS) // SUBLANE) * SUBLANE


def _cparams(sem, flags=None):
    return pltpu.CompilerParams(dimension_semantics=sem, vmem_limit_bytes=VMEM_LIMIT, flags=flags)


def _sigmoid(x):
    return 1.0 / (1.0 + jnp.exp(-x))


def _gelu(x):
    return 0.5 * x * (1.0 + jnp.tanh(math.sqrt(2.0 / math.pi) * (x + 0.044715 * (x * x * x))))


def _layer_norm(x, g, b):
    mu = jnp.mean(x, axis=-1, keepdims=True)
    xc = x - mu
    var = jnp.mean(xc * xc, axis=-1, keepdims=True)
    return xc * lax.rsqrt(var + LN_EPS) * g + b


def _rms_norm(x, g):
    return x * lax.rsqrt(jnp.mean(x * x, axis=-1, keepdims=True) + RMS_EPS) * g


def _dot(a, b):
    return jnp.dot(a, b, preferred_element_type=F32)


def _dot_nt(a, b):
    return lax.dot_general(a, b, (((1,), (1,)), ((), ())), preferred_element_type=F32)


def _in_proj_kernel(*refs, pre_ln):
    if pre_ln:
        (x_ref, g_ref, b_ref, w_ref, qg_ref, wuq_ref, kvg_ref, wukv_ref, rc_ref, rs1_ref, rs2_ref,
         h_ref, zc_ref, zs_ref, q_ref, k_ref, v_ref) = refs
    else:
        (x_ref, w_ref, qg_ref, wuq_ref, kvg_ref, wukv_ref, rc_ref, rs1_ref, rs2_ref,
         zc_ref, zs_ref, q_ref, k_ref, v_ref) = refs
    x = x_ref[...]
    if pre_ln:
        x = _layer_norm(x, g_ref[...], b_ref[...])
        h_ref[...] = x
    z = _dot(x.astype(BF16), w_ref[...])
    zc_ref[...] = z[:, 0:512]
    zs_ref[...] = z[:, 512:768]
    cq = _rms_norm(z[:, 768:1280], qg_ref[...])
    ckv = _rms_norm(z[:, 1280:1536], kvg_ref[...])
    kr = z[:, 1536:1664]
    q = _dot(cq.astype(BF16), wuq_ref[...])
    kv = _dot(ckv.astype(BF16), wukv_ref[...])
    rc, rs1, rs2 = rc_ref[...], rs1_ref[...], rs2_ref[...]

    def rope(t):
        return t * rc + pltpu.roll(t, 96, 1) * rs1 + pltpu.roll(t, 32, 1) * rs2

    scale = MLA_DK ** -0.5
    k_rope = rope(kr).astype(BF16)
    for h in range(MLA_HEADS):
        base = h * MLA_DKP
        q_ref[h, :, 0:128] = (q[:, base:base + 128] * scale).astype(BF16)
        q_ref[h, :, 128:256] = (rope(q[:, base + 128:base + 256]) * scale).astype(BF16)
        k_ref[h, :, 0:128] = kv[:, h * 128:(h + 1) * 128].astype(BF16)
        k_ref[h, :, 128:256] = k_rope
        v_ref[h] = kv[:, 512 + h * 128:512 + (h + 1) * 128].astype(BF16)


def _in_proj(x, lw, rope_tabs, ln=None):
    B, L, D = x.shape
    TM = min(512, L)
    nt = L // TM
    pre_ln = ln is not None
    row = lambda b, j: (b, j, 0)
    const2 = lambda b, j: (0, 0)
    in_specs = [pl.BlockSpec((None, TM, D), row)]
    args = [x]
    if pre_ln:
        in_specs += [pl.BlockSpec((1, D), const2)] * 2
        args += [ln[0], ln[1]]
    in_specs += [
        pl.BlockSpec((D, IN_PAD), const2),
        pl.BlockSpec((1, MLA_Q_RANK), const2),
        pl.BlockSpec((MLA_Q_RANK, MLA_HEADS * MLA_DKP), const2),
        pl.BlockSpec((1, MLA_KV_RANK), const2),
        pl.BlockSpec((MLA_KV_RANK, 2 * MLA_HEADS * 128), const2),
    ] + [pl.BlockSpec((TM, LANE), lambda b, j: (j, 0))] * 3
    args += [lw["w_in"], lw["q_norm_g"], lw["w_uq"], lw["kv_norm_g"], lw["w_ukv"], *rope_tabs]
    out_shape, out_specs = [], []
    if pre_ln:
        out_shape.append(jax.ShapeDtypeStruct((B, L, D), F32))
        out_specs.append(pl.BlockSpec((None, TM, D), row))
    out_shape += [
        jax.ShapeDtypeStruct((B, L, 2 * CONV_CH), F32),
        jax.ShapeDtypeStruct((L, B * SSM_CH), F32),
        jax.ShapeDtypeStruct((B, MLA_HEADS, L, MLA_DKP), BF16),
        jax.ShapeDtypeStruct((B, MLA_HEADS, L, MLA_DKP), BF16),
        jax.ShapeDtypeStruct((B, MLA_HEADS, L, MLA_V), BF16),
    ]
    hrow = lambda b, j: (b, 0, j, 0)
    out_specs += [
        pl.BlockSpec((None, TM, 2 * CONV_CH), row),
        pl.BlockSpec((TM, SSM_CH), lambda b, j: (j, b)),
        pl.BlockSpec((None, MLA_HEADS, TM, MLA_DKP), hrow),
        pl.BlockSpec((None, MLA_HEADS, TM, MLA_DKP), hrow),
        pl.BlockSpec((None, MLA_HEADS, TM, MLA_V), hrow),
    ]
    return pl.pallas_call(
        functools.partial(_in_proj_kernel, pre_ln=pre_ln),
        grid=(B, nt), in_specs=in_specs, out_specs=out_specs, out_shape=out_shape,
        compiler_params=_cparams(("parallel", "parallel")), name="in_proj",
    )(*args)


def _attn_kernel(q_ref, k_ref, v_ref, o_ref, m_ref, l_ref, acc_ref, s_ref, p_ref, *, nk, tk):
    j = pl.program_id(3)

    @pl.when(j == 0)
    def _():
        m_ref[...] = jnp.full(m_ref.shape, -jnp.inf, F32)
        l_ref[...] = jnp.zeros(l_ref.shape, F32)
        acc_ref[...] = jnp.zeros(acc_ref.shape, F32)

    s_ref[...] = _dot_nt(q_ref[...], k_ref[...])
    chunks = [pl.ds(c * LANE, LANE) for c in range(tk // LANE)]
    m_part = s_ref[:, chunks[0]]
    for ch in chunks[1:]:
        m_part = jnp.maximum(m_part, s_ref[:, ch])
    m_prev = m_ref[...]
    m_new = jnp.maximum(m_prev, jnp.max(m_part, axis=1, keepdims=True))
    alpha = jnp.exp(m_prev - m_new)
    l_part = jnp.zeros(m_new.shape, F32)
    for ch in chunks:
        p = jnp.exp(s_ref[:, ch] - m_new)
        p_ref[:, ch] = p.astype(BF16)
        l_part = l_part + p
    l_ref[...] = alpha * l_ref[...] + jnp.sum(l_part, axis=1, keepdims=True)
    acc_ref[...] = alpha * acc_ref[...] + _dot(p_ref[...], v_ref[...])
    m_ref[...] = m_new

    @pl.when(j == nk - 1)
    def _():
        o_ref[...] = (acc_ref[...] / l_ref[...]).astype(o_ref.dtype)


def _attention(q, k, v):
    B, H, L, _ = q.shape
    TQ = min(1024, L)
    TK = min(1024, L)
    nq, nk = L // TQ, L // TK
    return pl.pallas_call(
        functools.partial(_attn_kernel, nk=nk, tk=TK),
        grid=(B, H, nq, nk),
        in_specs=[
            pl.BlockSpec((None, None, TQ, MLA_DKP), lambda b, h, i, j: (b, h, i, 0)),
            pl.BlockSpec((None, None, TK, MLA_DKP), lambda b, h, i, j: (b, h, j, 0)),
            pl.BlockSpec((None, None, TK, MLA_V), lambda b, h, i, j: (b, h, j, 0)),
        ],
        out_specs=pl.BlockSpec((None, TQ, MLA_V), lambda b, h, i, j: (b, i, h)),
        out_shape=jax.ShapeDtypeStruct((B, L, H * MLA_V), BF16),
        scratch_shapes=[pltpu.VMEM((TQ, LANE), F32), pltpu.VMEM((TQ, LANE), F32),
                        pltpu.VMEM((TQ, MLA_V), F32), pltpu.VMEM((TQ, TK), F32), pltpu.VMEM((TQ, TK), BF16)],
        compiler_params=_cparams(("parallel", "parallel", "parallel", "arbitrary")), name="attn",
    )(q, k, v)


def _conv_kernel(zm_ref, zp_ref, zn_ref, w_ref, cb_ref, g_ref, b_ref, o_ref, hp_ref, *, tc, nt, rc):
    j = pl.program_id(1)

    def glu(z):
        return z[:, :CONV_CH] * _sigmoid(z[:, CONV_CH:])

    hp_ref[0:CONV_HALO, :] = jnp.where(j > 0, glu(zp_ref[...]), 0.0)
    hp_ref[CONV_HALO:CONV_HALO + tc, :] = glu(zm_ref[...])
    hp_ref[CONV_HALO + tc:2 * CONV_HALO + tc, :] = jnp.where(j < nt - 1, glu(zn_ref[...]), 0.0)
    shift = CONV_HALO - CONV_WIDTH // 2
    for r in range(tc // rc):
        acc = jnp.zeros((rc, CONV_CH), F32)
        for kk in range(CONV_WIDTH):
            lo = r * rc + kk + shift
            acc = acc + hp_ref[lo:lo + rc, :] * w_ref[kk:kk + 1, :]
        y = _layer_norm(acc + cb_ref[...], g_ref[...], b_ref[...])
        o_ref[r * rc:(r + 1) * rc, :] = (y * _sigmoid(y)).astype(o_ref.dtype)


def _conv(zc, lw):
    B, L, _ = zc.shape
    TC = min(512, L)
    nt = L // TC
    hb = TC // CONV_HALO
    nhb = L // CONV_HALO
    const2 = lambda b, j: (0, 0)
    return pl.pallas_call(
        functools.partial(_conv_kernel, tc=TC, nt=nt, rc=min(64, TC)),
        grid=(B, nt),
        in_specs=[
            pl.BlockSpec((None, TC, 2 * CONV_CH), lambda b, j: (b, j, 0)),
            pl.BlockSpec((None, CONV_HALO, 2 * CONV_CH), lambda b, j: (b, jnp.maximum(j * hb - 1, 0), 0)),
            pl.BlockSpec((None, CONV_HALO, 2 * CONV_CH), lambda b, j: (b, jnp.minimum((j + 1) * hb, nhb - 1), 0)),
            pl.BlockSpec((CONV_WIDTH, CONV_CH), const2),
            pl.BlockSpec((1, CONV_CH), const2),
            pl.BlockSpec((1, CONV_CH), const2),
            pl.BlockSpec((1, CONV_CH), const2),
        ],
        out_specs=pl.BlockSpec((None, TC, CONV_CH), lambda b, j: (b, j, 0)),
        out_shape=jax.ShapeDtypeStruct((B, L, CONV_CH), BF16),
        scratch_shapes=[pltpu.VMEM((TC + 2 * CONV_HALO, CONV_CH), F32)],
        compiler_params=_cparams(("parallel", "parallel")), name="conv",
    )(zc, zc, zc, lw["conv_w"], lw["conv_b"], lw["conv_ln_g"], lw["conv_ln_b"])


def _ssm_kernel(uf_ref, ub_ref, bm_ref, coef_ref, cm_ref, yf_ref, yb_ref,
                sf_ref, sb_ref, hf_ref, hb_ref, *, nb, steps, lc):
    i = pl.program_id(0)
    S = SSM_LANES

    @pl.when(i == 0)
    def _():
        hf_ref[...] = jnp.zeros(hf_ref.shape, F32)
        hb_ref[...] = jnp.zeros(hb_ref.shape, F32)

    sf_ref[...] = _dot(uf_ref[...].astype(BF16), bm_ref[0])
    sb_ref[...] = _dot(ub_ref[...].astype(BF16), bm_ref[1])

    lo_half = lax.broadcasted_iota(jnp.int32, (SUBLANE, lc), 0) < nb

    for c0 in range(0, S, lc):
        re_l = pl.ds(c0, lc)
        im_l = pl.ds(S + c0, lc)
        cf = [coef_ref[0, k, :, c0:c0 + lc] for k in range(4)]
        cb = [coef_ref[1, k, :, c0:c0 + lc] for k in range(4)]

        def body(t, carry, re_l=re_l, im_l=im_l, cf=cf, cb=cb):
            hfr, hfi, hbr, hbi = carry
            rf = pl.ds(pl.multiple_of(t * SUBLANE, SUBLANE), SUBLANE)
            rb = pl.ds(pl.multiple_of((steps - 1 - t) * SUBLANE, SUBLANE), SUBLANE)
            out = []
            for ref, rows, c, hr, hi, fwd in ((sf_ref, rf, cf, hfr, hfi, True), (sb_ref, rb, cb, hbr, hbi, False)):
                xr = ref[rows, re_l]
                xi = ref[rows, im_l]
                nr = c[0] * hr - c[1] * hi + xr
                ni = c[0] * hi + c[1] * hr + xi
                if nb != SUBLANE:
                    sxr = pltpu.roll(xr, nb, 0)
                    sxi = pltpu.roll(xi, nb, 0)
                    nr = nr + (c[2] * sxr - c[3] * sxi)
                    ni = ni + (c[2] * sxi + c[3] * sxr)
                ref[rows, re_l] = nr
                ref[rows, im_l] = ni
                if nb != SUBLANE:
                    snr = pltpu.roll(nr, nb, 0)
                    sni = pltpu.roll(ni, nb, 0)
                    if fwd:
                        nr = jnp.where(lo_half, snr, nr)
                        ni = jnp.where(lo_half, sni, ni)
                    else:
                        nr = jnp.where(lo_half, nr, snr)
                        ni = jnp.where(lo_half, ni, sni)
                out += [nr, ni]
            return tuple(out)

        init = (hf_ref[:, re_l], hf_ref[:, im_l], hb_ref[:, re_l], hb_ref[:, im_l])
        hfr, hfi, hbr, hbi = lax.fori_loop(0, steps, body, init, unroll=2)
        hf_ref[:, re_l] = hfr
        hf_ref[:, im_l] = hfi
        hb_ref[:, re_l] = hbr
        hb_ref[:, im_l] = hbi

    yf_ref[...] = _dot(sf_ref[...].astype(BF16), cm_ref[0])
    yb_ref[...] = _dot(sb_ref[...].astype(BF16), cm_ref[1])


def _ssm(u_rows, lw, nb):
    N = u_rows.shape[0]
    R = min(1024, N)
    n = N // R
    steps = R // SUBLANE
    S2 = 2 * SSM_LANES
    return pl.pallas_call(
        functools.partial(_ssm_kernel, nb=nb, steps=steps, lc=512),
        grid=(n,),
        in_specs=[
            pl.BlockSpec((R, SSM_CH), lambda i: (i, 0)),
            pl.BlockSpec((R, SSM_CH), lambda i: (n - 1 - i, 0)),
            pl.BlockSpec((2, SSM_CH, S2), lambda i: (0, 0, 0)),
            pl.BlockSpec((2, 4, SUBLANE, SSM_LANES), lambda i: (0, 0, 0, 0)),
            pl.BlockSpec((2, S2, SSM_CH), lambda i: (0, 0, 0)),
        ],
        out_specs=[
            pl.BlockSpec((R, SSM_CH), lambda i: (i, 0)),
            pl.BlockSpec((R, SSM_CH), lambda i: (n - 1 - i, 0)),
        ],
        out_shape=[jax.ShapeDtypeStruct((N, SSM_CH), F32)] * 2,
        scratch_shapes=[pltpu.VMEM((R, S2), F32), pltpu.VMEM((R, S2), F32),
                        pltpu.VMEM((SUBLANE, S2), F32), pltpu.VMEM((SUBLANE, S2), F32)],
        compiler_params=_cparams(("arbitrary",)), name="ssm",
    )(u_rows, u_rows, lw["ssm_bm"], lw["ssm_coef" + str(nb)], lw["ssm_cm"])


def _mix_kernel(h_ref, oc_ref, u_ref, yf_ref, yb_ref, om_ref, d_ref, gw_ref, gb_ref, wo_ref, g_ref, b_ref,
                h1_ref, h1b_ref, h1t_ref):
    y = d_ref[...] * u_ref[...] + yf_ref[...] + yb_ref[...]
    g = _gelu(y)
    o_ssm = g * _sigmoid(_dot(g.astype(BF16), gw_ref[...]) + gb_ref[...])
    mix = (_dot(oc_ref[...], wo_ref[0:256, :]) + _dot(o_ssm.astype(BF16), wo_ref[256:512, :])
           + _dot(om_ref[...], wo_ref[512:1024, :]))
    h1 = _layer_norm(DEEPNORM_ALPHA * h_ref[...] + mix, g_ref[...], b_ref[...])
    h1_ref[...] = h1
    h1b_ref[...] = h1.astype(BF16)
    h1t_ref[...] = h1.T.astype(BF16)


def _mix(h, o_conv, zs, yf, yb, o_mla, lw):
    B, L, D = h.shape
    TM = min(512, L)
    nt = L // TM
    row = lambda b, j: (b, j, 0)
    tb = lambda b, j: (j, b)
    const2 = lambda b, j: (0, 0)
    return pl.pallas_call(
        _mix_kernel,
        grid=(B, nt),
        in_specs=[
            pl.BlockSpec((None, TM, D), row),
            pl.BlockSpec((None, TM, CONV_CH), row),
            pl.BlockSpec((TM, SSM_CH), tb),
            pl.BlockSpec((TM, SSM_CH), tb),
            pl.BlockSpec((TM, SSM_CH), tb),
            pl.BlockSpec((None, TM, MLA_HEADS * MLA_V), row),
            pl.BlockSpec((1, SSM_CH), const2),
            pl.BlockSpec((SSM_CH, SSM_CH), const2),
            pl.BlockSpec((1, SSM_CH), const2),
            pl.BlockSpec((D, D), const2),
            pl.BlockSpec((1, D), const2),
            pl.BlockSpec((1, D), const2),
        ],
        out_specs=[pl.BlockSpec((None, TM, D), row), pl.BlockSpec((None, TM, D), row),
                   pl.BlockSpec((D, TM), lambda b, j: (0, b * nt + j))],
        out_shape=[jax.ShapeDtypeStruct((B, L, D), F32), jax.ShapeDtypeStruct((B, L, D), BF16),
                   jax.ShapeDtypeStruct((D, B * L), BF16)],
        compiler_params=_cparams(("parallel", "parallel")), name="mix",
    )(h, o_conv, zs, yf, yb, o_mla, lw["ssm_d"], lw["ssm_glu_w"], lw["ssm_glu_b"], lw["w_out"],
      lw["ln1_g"], lw["ln1_b"])


def _top_desc(s, n):
    m = jnp.max(s, axis=0, keepdims=True)
    out = [m]
    for _ in range(n - 1):
        m = jnp.max(jnp.where(s < m, s, -jnp.inf), axis=0, keepdims=True)
        out.append(m)
    return out


def _oddeven_merge_sort_pairs(n):
    pairs, p = [], 1
    while p < n:
        k = p
        while k >= 1:
            for j in range(k % p, n - k, 2 * k):
                for i in range(min(k, n - j - k)):
                    if (i + j) // (2 * p) == (i + j + k) // (2 * p):
                        pairs.append((i + j, i + j + k))
            k //= 2
        p *= 2
    return tuple(pairs)


def _top_desc_keys(s, n):
    groups = s.shape[0] // SUBLANE
    r = [s[SUBLANE * i:SUBLANE * (i + 1), :] for i in range(groups)]
    for a, b in _oddeven_merge_sort_pairs(groups):
        r[a], r[b] = jnp.maximum(r[a], r[b]), jnp.minimum(r[a], r[b])
    ninf = jnp.full(r[0].shape, -jnp.inf, F32)
    out = []
    for k in range(n):
        m = jnp.max(r[0], axis=0, keepdims=True)
        out.append(m)
        if k == n - 1:
            break
        won = r[0] == m
        for i in range(min(groups, n - 1 - k)):
            r[i] = jnp.where(won, r[i + 1] if i + 1 < groups else ninf, r[i])
    return out


def _split_bf16(x):
    hi = x.astype(BF16)
    return hi, (x - hi.astype(F32)).astype(BF16)


def _route_kernel(x_ref, wq_ref, k1_ref, k2_ref, ns1_ref, e1_ref, s2m_ref, e2_ref, cand_ref):
    q = _dot(x_ref[...], wq_ref[...])
    k_hl = [_split_bf16(k1_ref[...]), _split_bf16(k2_ref[...])]
    cand_ref[...] = jnp.full(cand_ref.shape, -jnp.inf, F32)
    for h in range(PEER_HEADS):
        s, tops = [], []
        for half in range(2):
            lo = h * 2 * PEER_HALF + half * PEER_HALF
            q_hi, q_lo = _split_bf16(q[:, lo:lo + PEER_HALF])
            k_hi, k_lo = k_hl[half]
            sc = _dot_nt(k_hi, q_hi) + (_dot_nt(k_hi, q_lo) + _dot_nt(k_lo, q_hi))
            s.append(sc)
            tops.append(_top_desc_keys(sc, _CAND_N))
        for r, (a, b) in enumerate(_CAND_PAIRS):
            cand_ref[r:r + 1, :] = tops[0][a] + tops[1][b]
        cs = _top_desc(cand_ref[...], _CAND_N)
        thr = 0.5 * (cs[PEER_TOPK - 1] + cs[PEER_TOPK])
        z = jnp.ones_like(thr)
        for kk in range(1, PEER_TOPK):
            z = z + jnp.exp(cs[kk] - cs[0])
        outs = ((ns1_ref, -s[0]), (e1_ref, jnp.exp(s[0] - tops[0][0]) / z),
                (s2m_ref, s[1] - thr), (e2_ref, jnp.exp(s[1] - tops[1][0])))
        for ref, val in outs:
            for c in range(val.shape[1] // LANE):
                ref[h, c] = val[:, c * LANE:(c + 1) * LANE]


def _route(xb, lw):
    T, D = xb.shape
    TT = min(256, T)
    aux = jax.ShapeDtypeStruct((PEER_HEADS, T // LANE, PEER_NKEYS, LANE), F32)
    aux_spec = pl.BlockSpec((PEER_HEADS, TT // LANE, PEER_NKEYS, LANE), lambda i: (0, i, 0, 0))
    return pl.pallas_call(
        _route_kernel,
        grid=(T // TT,),
        in_specs=[
            pl.BlockSpec((TT, D), lambda i: (i, 0)),
            pl.BlockSpec((D, 2 * PEER_HEADS * PEER_HALF), lambda i: (0, 0)),
            pl.BlockSpec((PEER_NKEYS, PEER_HALF), lambda i: (0, 0)),
            pl.BlockSpec((PEER_NKEYS, PEER_HALF), lambda i: (0, 0)),
        ],
        out_specs=[aux_spec] * 4,
        out_shape=[aux] * 4,
        scratch_shapes=[pltpu.VMEM((_CAND_ROWS, TT), F32)],
        compiler_params=_cparams(("parallel",)), name="route",
    )(xb, lw["peer_w_q"], lw["peer_k1"], lw["peer_k2"])


MXU_TILE = 256


def _peer_kernel(xt_ref, u_ref, vt_ref, ns1_ref, e1_ref, s2m_ref, e2_ref, o_ref, act_ref, ag_ref, acc_ref,
                 *, ne, eb, tt):
    e = pl.program_id(1)

    @pl.when(e == 0)
    def _():
        acc_ref[...] = jnp.zeros(acc_ref.shape, F32)

    n1 = eb // PEER_NKEYS
    rows = pl.ds(pl.multiple_of(e * n1, n1), n1)
    per_tile = MXU_TILE // LANE
    for tn in range(tt // MXU_TILE):
        tcols = pl.ds(tn * MXU_TILE, MXU_TILE)
        for tk in range(eb // MXU_TILE):
            trows = pl.ds(tk * MXU_TILE, MXU_TILE)
            act_ref[trows, tcols] = _dot(u_ref[trows, :], xt_ref[:, tcols])
            for c in range(tn * per_tile, (tn + 1) * per_tile):
                cols = pl.ds(c * LANE, LANE)
                ns = [ns1_ref[h, c, rows, :] for h in range(PEER_HEADS)]
                ee = [e1_ref[h, c, rows, :] for h in range(PEER_HEADS)]
                for i in range(tk * per_tile, (tk + 1) * per_tile):
                    gate = jnp.zeros((PEER_NKEYS, LANE), F32)
                    for h in range(PEER_HEADS):
                        sel = s2m_ref[h, c] > ns[h][i:i + 1, :]
                        gate = gate + jnp.where(sel, e2_ref[h, c] * ee[h][i:i + 1, :], 0.0)
                    er = pl.ds(i * PEER_NKEYS, PEER_NKEYS)
                    ag_ref[er, cols] = (_gelu(act_ref[er, cols]) * gate).astype(BF16)
    acc_ref[...] += _dot(vt_ref[...], ag_ref[...])

    @pl.when(e == ne - 1)
    def _():
        o_ref[...] = acc_ref[...].T


def _peer(xt, aux, lw):
    D, T = xt.shape
    TT = min(512, T)
    EB = 2048
    ne = PEER_EXPERTS // EB
    aux_spec = pl.BlockSpec((PEER_HEADS, TT // LANE, PEER_NKEYS, LANE), lambda i, e: (0, i, 0, 0))
    return pl.pallas_call(
        functools.partial(_peer_kernel, ne=ne, eb=EB, tt=TT),
        grid=(T // TT, ne),
        in_specs=[
            pl.BlockSpec((D, TT), lambda i, e: (0, i)),
            pl.BlockSpec((EB, D), lambda i, e: (e, 0)),
            pl.BlockSpec((D, EB), lambda i, e: (0, e)),
        ] + [aux_spec] * 4,
        out_specs=pl.BlockSpec((TT, D), lambda i, e: (i, 0)),
        out_shape=jax.ShapeDtypeStruct((T, D), F32),
        scratch_shapes=[pltpu.VMEM((EB, TT), F32), pltpu.VMEM((EB, TT), BF16), pltpu.VMEM((D, TT), F32)],
        compiler_params=_cparams(("parallel", "arbitrary")), name="peer",
    )(xt, lw["peer_u"], lw["peer_vt"], *aux)


def _ple_kernel(h1_ref, h1b_ref, pe_ref, p_ref, wg_ref, wp_ref, g_ref, b_ref, o_ref):
    gate = _sigmoid(_dot(h1b_ref[...], wg_ref[...]))
    emb = _dot(p_ref[...].astype(BF16), wp_ref[...])
    r = DEEPNORM_ALPHA * h1_ref[...] + pe_ref[...] + gate * emb
    o_ref[...] = _layer_norm(r, g_ref[...], b_ref[...])


def _ple(h1, h1b, peer_out, p, lw):
    T, D = h1.shape
    TM = min(512, T)
    row = lambda i: (i, 0)
    const = lambda i: (0, 0)
    return pl.pallas_call(
        _ple_kernel,
        grid=(T // TM,),
        in_specs=[
            pl.BlockSpec((TM, D), row), pl.BlockSpec((TM, D), row), pl.BlockSpec((TM, D), row),
            pl.BlockSpec((TM, PLE_DIM), row),
            pl.BlockSpec((D, D), const), pl.BlockSpec((PLE_DIM, D), const),
            pl.BlockSpec((1, D), const), pl.BlockSpec((1, D), const),
        ],
        out_specs=pl.BlockSpec((TM, D), row),
        out_shape=jax.ShapeDtypeStruct((T, D), F32),
        compiler_params=_cparams(("parallel",)), name="ple",
    )(h1, h1b, peer_out, p, lw["ple_w_g"], lw["ple_w_p"], lw["ln2_g"], lw["ln2_b"])


def _ssm_params(a_re, a_im, b_re, b_im, c_re, c_im, log_dt):
    G, P, C = SSM_GROUPS, SSM_STATE, SSM_GROUP
    ar, ai = a_re.astype(F32), a_im.astype(F32)
    dt = jnp.exp(log_dt.astype(F32))[..., None]
    mag = jnp.exp(dt * ar)
    abr, abi = mag * jnp.cos(dt * ai), mag * jnp.sin(dt * ai)
    den = ar * ar + ai * ai
    qr = ((abr - 1.0) * ar + abi * ai) / den
    qi = (abi * ar - (abr - 1.0) * ai) / den
    br, bi = b_re.astype(F32), b_im.astype(F32)
    bbr = qr[..., None] * br - qi[..., None] * bi
    bbi = qr[..., None] * bi + qi[..., None] * br
    eye = jnp.eye(G, dtype=F32)

    def bdiag_in(m):
        return jnp.einsum("dgpc,gh->dgchp", m, eye).reshape(2, G * C, G * P)

    def bdiag_out(m):
        return jnp.einsum("dgcp,gh->dgphc", m, eye).reshape(2, G * P, G * C)

    bm = jnp.concatenate([bdiag_in(bbr), bdiag_in(bbi)], axis=2).astype(BF16)
    cm = jnp.concatenate([bdiag_out(c_re.astype(F32)), -bdiag_out(c_im.astype(F32))], axis=1).astype(BF16)
    a1 = (abr.reshape(2, G * P), abi.reshape(2, G * P))
    a2 = (a1[0] * a1[0] - a1[1] * a1[1], 2.0 * a1[0] * a1[1])
    zero = (jnp.zeros_like(a1[0]), jnp.zeros_like(a1[0]))
    pick = lambda z, d: (z[0][d:d + 1], z[1][d:d + 1])

    def rows(top, bottom):
        shape = (top[0].shape[0], SUBLANE // 2, G * P)
        return [jnp.concatenate([jnp.broadcast_to(t[:, None, :], shape),
                                 jnp.broadcast_to(b[:, None, :], shape)], axis=1) for t, b in zip(top, bottom)]

    c8 = jnp.stack([*rows(a1, a1), *rows(zero, zero)], axis=1)
    f1, f2 = rows(pick(a1, 0), pick(a2, 0)), rows(pick(zero, 0), pick(a1, 0))
    b1, b2 = rows(pick(a2, 1), pick(a1, 1)), rows(pick(a1, 1), pick(zero, 1))
    c4 = jnp.concatenate([jnp.stack([*f1, *f2], axis=1), jnp.stack([*b1, *b2], axis=1)], axis=0)
    return bm, cm, c8.astype(F32), c4.astype(F32)


def _rope_tables(L):
    pos = jnp.arange(L, dtype=F32)
    inv_freq = ROPE_THETA ** (-jnp.arange(0, MLA_ROPE, 2, dtype=F32) / MLA_ROPE)
    ang = pos[:, None] * inv_freq[None, :]
    cos, sin = jnp.cos(ang), jnp.sin(ang)
    z = jnp.zeros_like(cos)
    return (jnp.concatenate([cos, cos, z, z], axis=1),
            jnp.concatenate([-sin, z, z, z], axis=1),
            jnp.concatenate([z, sin, z, z], axis=1))


def _layer_weights(i, w_in, conv_w, conv_b, conv_ln_g, conv_ln_b,
                   ssm_a_re, ssm_a_im, ssm_b_re, ssm_b_im, ssm_c_re, ssm_c_im, ssm_log_dt, ssm_d, ssm_glu_w,
                   ssm_glu_b, mla_q_norm_g, mla_w_uq, mla_kv_norm_g, mla_w_ukv, w_out, ln1_g, ln1_b,
                   peer_w_q, peer_k1, peer_k2, peer_u, peer_v, ple_w_p, ple_w_g, ln2_g, ln2_b):
    row = lambda v: v[i].reshape(1, -1).astype(F32)
    H = MLA_HEADS
    wuq = mla_w_uq[i].reshape(MLA_Q_RANK, H, MLA_DK)
    wuq = jnp.pad(wuq, ((0, 0), (0, 0), (0, MLA_DKP - MLA_DK))).reshape(MLA_Q_RANK, H * MLA_DKP)
    wukv = mla_w_ukv[i].reshape(MLA_KV_RANK, H, MLA_NOPE + MLA_V)
    wukv = jnp.concatenate([wukv[:, :, :MLA_NOPE].reshape(MLA_KV_RANK, H * MLA_NOPE),
                            wukv[:, :, MLA_NOPE:].reshape(MLA_KV_RANK, H * MLA_V)], axis=1)
    bm, cm, c8, c4 = _ssm_params(ssm_a_re[i], ssm_a_im[i], ssm_b_re[i], ssm_b_im[i], ssm_c_re[i], ssm_c_im[i],
                                 ssm_log_dt[i])
    return {
        "w_in": jnp.pad(w_in[i], ((0, 0), (0, IN_PAD - w_in.shape[2]))).astype(BF16),
        "q_norm_g": row(mla_q_norm_g), "w_uq": wuq.astype(BF16),
        "kv_norm_g": row(mla_kv_norm_g), "w_ukv": wukv.astype(BF16),
        "conv_w": conv_w[i].astype(F32), "conv_b": row(conv_b),
        "conv_ln_g": row(conv_ln_g), "conv_ln_b": row(conv_ln_b),
        "ssm_bm": bm, "ssm_cm": cm, "ssm_coef8": c8, "ssm_coef4": c4,
        "ssm_d": row(ssm_d), "ssm_glu_w": ssm_glu_w[i].astype(BF16), "ssm_glu_b": row(ssm_glu_b),
        "w_out": w_out[i].astype(BF16), "ln1_g": row(ln1_g), "ln1_b": row(ln1_b),
        "peer_w_q": peer_w_q[i].astype(BF16), "peer_k1": peer_k1[i].astype(F32), "peer_k2": peer_k2[i].astype(F32),
        "peer_u": peer_u[i].astype(BF16), "peer_vt": peer_v[i].T.astype(BF16),
        "ple_w_p": ple_w_p[i].astype(BF16), "ple_w_g": ple_w_g[i].astype(BF16),
        "ln2_g": row(ln2_g), "ln2_b": row(ln2_b),
    }


def _run(x, p, layers, ln_emb):
    B, L, D = x.shape
    assert B in (SUBLANE // 2, SUBLANE), "the S5 scan packs 4 or 8 batch rows per sublane group"
    rope_tabs = _rope_tables(L)
    h = x
    for i, lw in enumerate(layers):
        if i == 0:
            h, zc, zs, q, k, v = _in_proj(h, lw, rope_tabs, ln=ln_emb)
        else:
            zc, zs, q, k, v = _in_proj(h, lw, rope_tabs)
        o_mla = _attention(q, k, v)
        o_conv = _conv(zc, lw)
        yf, yb = _ssm(zs.reshape(L * B, SSM_CH), lw, B)
        h1, h1b, h1t = _mix(h, o_conv, zs, yf.reshape(L, B * SSM_CH), yb.reshape(L, B * SSM_CH), o_mla, lw)
        h1 = h1.reshape(B * L, D)
        h1b = h1b.reshape(B * L, D)
        aux = _route(h1b, lw)
        peer_out = _peer(h1t, aux, lw)
        h = _ple(h1, h1b, peer_out, p[i].reshape(B * L, PLE_DIM), lw).reshape(B, L, D)
    return h


def kernel(x_prompt, x_sample, p_prompt, p_sample, ln_emb_g, ln_emb_b, w_in, conv_w, conv_b, conv_ln_g, conv_ln_b, ssm_a_re, ssm_a_im, ssm_b_re, ssm_b_im, ssm_c_re, ssm_c_im, ssm_log_dt, ssm_d, ssm_glu_w, ssm_glu_b, mla_q_norm_g, mla_w_uq, mla_kv_norm_g, mla_w_ukv, w_out, ln1_g, ln1_b, peer_w_q, peer_k1, peer_k2, peer_u, peer_v, ple_w_p, ple_w_g, ln2_g, ln2_b):
    layers = [
        _layer_weights(i, w_in, conv_w, conv_b, conv_ln_g, conv_ln_b,
                       ssm_a_re, ssm_a_im, ssm_b_re, ssm_b_im, ssm_c_re, ssm_c_im, ssm_log_dt, ssm_d, ssm_glu_w,
                       ssm_glu_b, mla_q_norm_g, mla_w_uq, mla_kv_norm_g, mla_w_ukv, w_out, ln1_g, ln1_b,
                       peer_w_q, peer_k1, peer_k2, peer_u, peer_v, ple_w_p, ple_w_g, ln2_g, ln2_b)
        for i in range(DEPTH)
    ]
    ln_emb = (ln_emb_g.reshape(1, -1).astype(F32), ln_emb_b.reshape(1, -1).astype(F32))
    y_prompt = _run(x_prompt, p_prompt, layers, ln_emb)
    y_sample = _run(x_sample, p_sample, layers, ln_emb)
    return (y_prompt, y_sample)
```

```python
import functools
import math

import jax
import jax.numpy as jnp
import numpy as np
from jax import lax
from jax.experimental import pallas as pl
from jax.experimental.pallas import tpu as pltpu

F32 = jnp.float32
BF16 = jnp.bfloat16

D_MODEL = 1024
DEPTH = 2
CONV_CH = 256
CONV_WIDTH = 31
CONV_HALO = 16
SSM_CH = 256
SSM_GROUP = 16
SSM_GROUPS = 16
SSM_STATE = 64
SSM_LANES = SSM_GROUPS * SSM_STATE
MLA_HEADS = 4
MLA_NOPE = 128
MLA_ROPE = 64
MLA_V = 128
MLA_DK = MLA_NOPE + MLA_ROPE
MLA_DKP = 256
MLA_Q_RANK = 512
MLA_KV_RANK = 256
ROPE_THETA = 10000.0
PEER_HEADS = 8
PEER_NKEYS = 128
PEER_EXPERTS = PEER_NKEYS * PEER_NKEYS
PEER_HALF = 128
PEER_TOPK = 16
PLE_DIM = 256
DEEPNORM_ALPHA = float((2 * DEPTH) ** 0.25)
LN_EPS = 1e-5
RMS_EPS = 1e-6
IN_PAD = 1664

LANE = 128
SUBLANE = 8
VMEM_LIMIT = 56 * 1024 * 1024

_CAND_N = PEER_TOPK + 1
_CAND_PAIRS = tuple((i, j) for i in range(_CAND_N) for j in range(_CAND_N) if (i + 1) * (j + 1) <= _CAND_N)
_CAND_ROWS = -(-len(_CAND_PAIRS) // SUBLANE) * SUBLANE


def _cparams(sem, flags=None):
    return pltpu.CompilerParams(dimension_semantics=sem, vmem_limit_bytes=VMEM_LIMIT, flags=flags)


def _sigmoid(x):
    return 1.0 / (1.0 + jnp.exp(-x))


def _gelu(x):
    c = math.sqrt(2.0 / math.pi)
    half = 0.5 * x
    return half + half * jnp.tanh(x * (c + (c * 0.044715) * (x * x)))


def _layer_norm(x, g, b):
    mu = jnp.mean(x, axis=-1, keepdims=True)
    xc = x - mu
    var = jnp.mean(xc * xc, axis=-1, keepdims=True)
    return xc * lax.rsqrt(var + LN_EPS) * g + b


def _rms_norm(x, g):
    return x * lax.rsqrt(jnp.mean(x * x, axis=-1, keepdims=True) + RMS_EPS) * g


def _dot(a, b):
    return jnp.dot(a, b, preferred_element_type=F32)


def _dot_nt(a, b):
    return lax.dot_general(a, b, (((1,), (1,)), ((), ())), preferred_element_type=F32)


def _in_proj_kernel(*refs, pre_ln):
    if pre_ln:
        (x_ref, g_ref, b_ref, w_ref, qg_ref, wuq_ref, kvg_ref, wukv_ref, rc_ref, rs1_ref, rs2_ref,
         h_ref, zc_ref, zs_ref, q_ref, k_ref, v_ref) = refs
    else:
        (x_ref, w_ref, qg_ref, wuq_ref, kvg_ref, wukv_ref, rc_ref, rs1_ref, rs2_ref,
         zc_ref, zs_ref, q_ref, k_ref, v_ref) = refs
    x = x_ref[...]
    if pre_ln:
        x = _layer_norm(x, g_ref[...], b_ref[...])
        h_ref[...] = x
    z = _dot(x.astype(BF16), w_ref[...])
    zc_ref[...] = z[:, 0:512]
    zs_ref[...] = z[:, 512:768]
    cq = _rms_norm(z[:, 768:1280], qg_ref[...])
    ckv = _rms_norm(z[:, 1280:1536], kvg_ref[...])
    kr = z[:, 1536:1664]
    q = _dot(cq.astype(BF16), wuq_ref[...])
    kv = _dot(ckv.astype(BF16), wukv_ref[...])
    rc, rs1, rs2 = rc_ref[...], rs1_ref[...], rs2_ref[...]

    def rope(t):
        return t * rc + pltpu.roll(t, 96, 1) * rs1 + pltpu.roll(t, 32, 1) * rs2

    scale = MLA_DK ** -0.5
    k_rope = rope(kr).astype(BF16)
    for h in range(MLA_HEADS):
        base = h * MLA_DKP
        q_ref[h, :, 0:128] = (q[:, base:base + 128] * scale).astype(BF16)
        q_ref[h, :, 128:256] = (rope(q[:, base + 128:base + 256]) * scale).astype(BF16)
        k_ref[h, :, 0:128] = kv[:, h * 128:(h + 1) * 128].astype(BF16)
        k_ref[h, :, 128:256] = k_rope
        v_ref[h] = kv[:, 512 + h * 128:512 + (h + 1) * 128].astype(BF16)


def _in_proj(x, lw, rope_tabs, ln=None):
    B, L, D = x.shape
    TM = min(512, L)
    nt = L // TM
    pre_ln = ln is not None
    row = lambda b, j: (b, j, 0)
    const2 = lambda b, j: (0, 0)
    in_specs = [pl.BlockSpec((None, TM, D), row)]
    args = [x]
    if pre_ln:
        in_specs += [pl.BlockSpec((1, D), const2)] * 2
        args += [ln[0], ln[1]]
    in_specs += [
        pl.BlockSpec((D, IN_PAD), const2),
        pl.BlockSpec((1, MLA_Q_RANK), const2),
        pl.BlockSpec((MLA_Q_RANK, MLA_HEADS * MLA_DKP), const2),
        pl.BlockSpec((1, MLA_KV_RANK), const2),
        pl.BlockSpec((MLA_KV_RANK, 2 * MLA_HEADS * 128), const2),
    ] + [pl.BlockSpec((TM, LANE), lambda b, j: (j, 0))] * 3
    args += [lw["w_in"], lw["q_norm_g"], lw["w_uq"], lw["kv_norm_g"], lw["w_ukv"], *rope_tabs]
    out_shape, out_specs = [], []
    if pre_ln:
        out_shape.append(jax.ShapeDtypeStruct((B, L, D), F32))
        out_specs.append(pl.BlockSpec((None, TM, D), row))
    out_shape += [
        jax.ShapeDtypeStruct((B, L, 2 * CONV_CH), F32),
        jax.ShapeDtypeStruct((L, B * SSM_CH), F32),
        jax.ShapeDtypeStruct((B, MLA_HEADS, L, MLA_DKP), BF16),
        jax.ShapeDtypeStruct((B, MLA_HEADS, L, MLA_DKP), BF16),
        jax.ShapeDtypeStruct((B, MLA_HEADS, L, MLA_V), BF16),
    ]
    hrow = lambda b, j: (b, 0, j, 0)
    out_specs += [
        pl.BlockSpec((None, TM, 2 * CONV_CH), row),
        pl.BlockSpec((TM, SSM_CH), lambda b, j: (j, b)),
        pl.BlockSpec((None, MLA_HEADS, TM, MLA_DKP), hrow),
        pl.BlockSpec((None, MLA_HEADS, TM, MLA_DKP), hrow),
        pl.BlockSpec((None, MLA_HEADS, TM, MLA_V), hrow),
    ]
    return pl.pallas_call(
        functools.partial(_in_proj_kernel, pre_ln=pre_ln),
        grid=(B, nt), in_specs=in_specs, out_specs=out_specs, out_shape=out_shape,
        compiler_params=_cparams(("parallel", "parallel")), name="in_proj",
    )(*args)


def _attn_kernel(q_ref, k_ref, v_ref, o_ref, m_ref, l_ref, acc_ref, s_ref, p_ref, *, nk, tk):
    j = pl.program_id(3)

    @pl.when(j == 0)
    def _():
        m_ref[...] = jnp.full(m_ref.shape, -jnp.inf, F32)
        l_ref[...] = jnp.zeros(l_ref.shape, F32)
        acc_ref[...] = jnp.zeros(acc_ref.shape, F32)

    s_ref[...] = _dot_nt(q_ref[...], k_ref[...])
    chunks = [pl.ds(c * LANE, LANE) for c in range(tk // LANE)]
    m_part = s_ref[:, chunks[0]]
    for ch in chunks[1:]:
        m_part = jnp.maximum(m_part, s_ref[:, ch])
    m_prev = m_ref[...]
    m_new = jnp.maximum(m_prev, jnp.max(m_part, axis=1, keepdims=True))
    alpha = jnp.exp(m_prev - m_new)
    l_part = jnp.zeros(m_new.shape, F32)
    for ch in chunks:
        p = jnp.exp(s_ref[:, ch] - m_new)
        p_ref[:, ch] = p.astype(BF16)
        l_part = l_part + p
    l_ref[...] = alpha * l_ref[...] + jnp.sum(l_part, axis=1, keepdims=True)
    acc_ref[...] = alpha * acc_ref[...] + _dot(p_ref[...], v_ref[...])
    m_ref[...] = m_new

    @pl.when(j == nk - 1)
    def _():
        o_ref[...] = (acc_ref[...] / l_ref[...]).astype(o_ref.dtype)


def _attention(q, k, v):
    B, H, L, _ = q.shape
    TQ = min(1024, L)
    TK = min(2048, L)
    nq, nk = L // TQ, L // TK
    return pl.pallas_call(
        functools.partial(_attn_kernel, nk=nk, tk=TK),
        grid=(B, H, nq, nk),
        in_specs=[
            pl.BlockSpec((None, None, TQ, MLA_DKP), lambda b, h, i, j: (b, h, i, 0)),
            pl.BlockSpec((None, None, TK, MLA_DKP), lambda b, h, i, j: (b, h, j, 0)),
            pl.BlockSpec((None, None, TK, MLA_V), lambda b, h, i, j: (b, h, j, 0)),
        ],
        out_specs=pl.BlockSpec((None, TQ, MLA_V), lambda b, h, i, j: (b, i, h)),
        out_shape=jax.ShapeDtypeStruct((B, L, H * MLA_V), BF16),
        scratch_shapes=[pltpu.VMEM((TQ, LANE), F32), pltpu.VMEM((TQ, LANE), F32),
                        pltpu.VMEM((TQ, MLA_V), F32), pltpu.VMEM((TQ, TK), F32), pltpu.VMEM((TQ, TK), BF16)],
        compiler_params=_cparams(("parallel", "parallel", "parallel", "arbitrary")), name="attn",
    )(q, k, v)


def _conv_kernel(zm_ref, zp_ref, zn_ref, w_ref, cb_ref, g_ref, b_ref, o_ref, hp_ref, sh_ref, *, tc, nt, rc):
    j = pl.program_id(1)

    def glu(z):
        return z[:, :CONV_CH] * _sigmoid(z[:, CONV_CH:])

    hp_ref[0:CONV_HALO, :] = jnp.where(j > 0, glu(zp_ref[...]), 0.0)
    hp_ref[CONV_HALO:CONV_HALO + tc, :] = glu(zm_ref[...])
    hp_ref[CONV_HALO + tc:2 * CONV_HALO + tc, :] = jnp.where(j < nt - 1, glu(zn_ref[...]), 0.0)
    shift = CONV_HALO - CONV_WIDTH // 2
    span = -(-(CONV_WIDTH + shift) // SUBLANE) * SUBLANE
    for a in range(SUBLANE):
        sh_ref[a] = hp_ref[a:a + tc + span - SUBLANE, :]
    for r in range(tc // rc):
        acc = jnp.zeros((rc, CONV_CH), F32)
        for a in range(SUBLANE):
            for mm in range(span // SUBLANE):
                kk = mm * SUBLANE + a - shift
                if 0 <= kk < CONV_WIDTH:
                    lo = r * rc + mm * SUBLANE
                    acc = acc + sh_ref[a, lo:lo + rc, :] * w_ref[kk:kk + 1, :]
        y = _layer_norm(acc + cb_ref[...], g_ref[...], b_ref[...])
        o_ref[r * rc:(r + 1) * rc, :] = (y * _sigmoid(y)).astype(o_ref.dtype)


def _conv(zc, lw):
    B, L, _ = zc.shape
    TC = min(512, L)
    nt = L // TC
    hb = TC // CONV_HALO
    nhb = L // CONV_HALO
    const2 = lambda b, j: (0, 0)
    return pl.pallas_call(
        functools.partial(_conv_kernel, tc=TC, nt=nt, rc=min(64, TC)),
        grid=(B, nt),
        in_specs=[
            pl.BlockSpec((None, TC, 2 * CONV_CH), lambda b, j: (b, j, 0)),
            pl.BlockSpec((None, CONV_HALO, 2 * CONV_CH), lambda b, j: (b, jnp.maximum(j * hb - 1, 0), 0)),
            pl.BlockSpec((None, CONV_HALO, 2 * CONV_CH), lambda b, j: (b, jnp.minimum((j + 1) * hb, nhb - 1), 0)),
            pl.BlockSpec((CONV_WIDTH, CONV_CH), const2),
            pl.BlockSpec((1, CONV_CH), const2),
            pl.BlockSpec((1, CONV_CH), const2),
            pl.BlockSpec((1, CONV_CH), const2),
        ],
        out_specs=pl.BlockSpec((None, TC, CONV_CH), lambda b, j: (b, j, 0)),
        out_shape=jax.ShapeDtypeStruct((B, L, CONV_CH), BF16),
        scratch_shapes=[pltpu.VMEM((TC + 2 * CONV_HALO, CONV_CH), F32),
                        pltpu.VMEM((SUBLANE, TC + 2 * CONV_HALO - SUBLANE, CONV_CH), F32)],
        compiler_params=_cparams(("parallel", "parallel")), name="conv",
    )(zc, zc, zc, lw["conv_w"], lw["conv_b"], lw["conv_ln_g"], lw["conv_ln_b"])


def _ssm_kernel(uf_ref, ub_ref, bm_ref, coef_ref, cm_ref, yf_ref, yb_ref,
                sf_ref, sb_ref, hf_ref, hb_ref, *, nb, steps, lc):
    i = pl.program_id(0)
    S = SSM_LANES

    @pl.when(i == 0)
    def _():
        hf_ref[...] = jnp.zeros(hf_ref.shape, F32)
        hb_ref[...] = jnp.zeros(hb_ref.shape, F32)

    sf_ref[...] = _dot(uf_ref[...].astype(BF16), bm_ref[0])
    sb_ref[...] = _dot(ub_ref[...].astype(BF16), bm_ref[1])

    lo_half = lax.broadcasted_iota(jnp.int32, (SUBLANE, lc), 0) < nb

    for c0 in range(0, S, lc):
        re_l = pl.ds(c0, lc)
        im_l = pl.ds(S + c0, lc)
        cf = [coef_ref[0, k, :, c0:c0 + lc] for k in range(4)]
        cb = [coef_ref[1, k, :, c0:c0 + lc] for k in range(4)]

        def body(t, carry, re_l=re_l, im_l=im_l, cf=cf, cb=cb):
            hfr, hfi, hbr, hbi = carry
            rf = pl.ds(pl.multiple_of(t * SUBLANE, SUBLANE), SUBLANE)
            rb = pl.ds(pl.multiple_of((steps - 1 - t) * SUBLANE, SUBLANE), SUBLANE)
            out = []
            for ref, rows, c, hr, hi, fwd in ((sf_ref, rf, cf, hfr, hfi, True), (sb_ref, rb, cb, hbr, hbi, False)):
                xr = ref[rows, re_l]
                xi = ref[rows, im_l]
                nr = c[0] * hr - c[1] * hi + xr
                ni = c[0] * hi + c[1] * hr + xi
                if nb != SUBLANE:
                    sxr = pltpu.roll(xr, nb, 0)
                    sxi = pltpu.roll(xi, nb, 0)
                    nr = nr + (c[2] * sxr - c[3] * sxi)
                    ni = ni + (c[2] * sxi + c[3] * sxr)
                ref[rows, re_l] = nr
                ref[rows, im_l] = ni
                if nb != SUBLANE:
                    snr = pltpu.roll(nr, nb, 0)
                    sni = pltpu.roll(ni, nb, 0)
                    if fwd:
                        nr = jnp.where(lo_half, snr, nr)
                        ni = jnp.where(lo_half, sni, ni)
                    else:
                        nr = jnp.where(lo_half, nr, snr)
                        ni = jnp.where(lo_half, ni, sni)
                out += [nr, ni]
            return tuple(out)

        init = (hf_ref[:, re_l], hf_ref[:, im_l], hb_ref[:, re_l], hb_ref[:, im_l])
        hfr, hfi, hbr, hbi = lax.fori_loop(0, steps, body, init, unroll=2)
        hf_ref[:, re_l] = hfr
        hf_ref[:, im_l] = hfi
        hb_ref[:, re_l] = hbr
        hb_ref[:, im_l] = hbi

    yf_ref[...] = _dot(sf_ref[...].astype(BF16), cm_ref[0])
    yb_ref[...] = _dot(sb_ref[...].astype(BF16), cm_ref[1])


def _ssm(u_rows, lw, nb):
    N = u_rows.shape[0]
    R = min(1024, N)
    n = N // R
    steps = R // SUBLANE
    S2 = 2 * SSM_LANES
    return pl.pallas_call(
        functools.partial(_ssm_kernel, nb=nb, steps=steps, lc=512),
        grid=(n,),
        in_specs=[
            pl.BlockSpec((R, SSM_CH), lambda i: (i, 0)),
            pl.BlockSpec((R, SSM_CH), lambda i: (n - 1 - i, 0)),
            pl.BlockSpec((2, SSM_CH, S2), lambda i: (0, 0, 0)),
            pl.BlockSpec((2, 4, SUBLANE, SSM_LANES), lambda i: (0, 0, 0, 0)),
            pl.BlockSpec((2, S2, SSM_CH), lambda i: (0, 0, 0)),
        ],
        out_specs=[
            pl.BlockSpec((R, SSM_CH), lambda i: (i, 0)),
            pl.BlockSpec((R, SSM_CH), lambda i: (n - 1 - i, 0)),
        ],
        out_shape=[jax.ShapeDtypeStruct((N, SSM_CH), F32)] * 2,
        scratch_shapes=[pltpu.VMEM((R, S2), F32), pltpu.VMEM((R, S2), F32),
                        pltpu.VMEM((SUBLANE, S2), F32), pltpu.VMEM((SUBLANE, S2), F32)],
        compiler_params=_cparams(("arbitrary",)), name="ssm",
    )(u_rows, u_rows, lw["ssm_bm"], lw["ssm_coef" + str(nb)], lw["ssm_cm"])


def _mix_kernel(h_ref, oc_ref, u_ref, yf_ref, yb_ref, om_ref, d_ref, gw_ref, gb_ref, wo_ref, g_ref, b_ref,
                h1_ref, h1b_ref, h1t_ref):
    y = d_ref[...] * u_ref[...] + yf_ref[...] + yb_ref[...]
    g = _gelu(y)
    o_ssm = g * _sigmoid(_dot(g.astype(BF16), gw_ref[...]) + gb_ref[...])
    mix = (_dot(oc_ref[...], wo_ref[0:256, :]) + _dot(o_ssm.astype(BF16), wo_ref[256:512, :])
           + _dot(om_ref[...], wo_ref[512:1024, :]))
    h1 = _layer_norm(DEEPNORM_ALPHA * h_ref[...] + mix, g_ref[...], b_ref[...])
    h1_ref[...] = h1
    h1b_ref[...] = h1.astype(BF16)
    h1t_ref[...] = h1.T.astype(BF16)


def _mix(h, o_conv, zs, yf, yb, o_mla, lw):
    B, L, D = h.shape
    TM = min(512, L)
    nt = L // TM
    row = lambda b, j: (b, j, 0)
    tb = lambda b, j: (j, b)
    const2 = lambda b, j: (0, 0)
    return pl.pallas_call(
        _mix_kernel,
        grid=(B, nt),
        in_specs=[
            pl.BlockSpec((None, TM, D), row),
            pl.BlockSpec((None, TM, CONV_CH), row),
            pl.BlockSpec((TM, SSM_CH), tb),
            pl.BlockSpec((TM, SSM_CH), tb),
            pl.BlockSpec((TM, SSM_CH), tb),
            pl.BlockSpec((None, TM, MLA_HEADS * MLA_V), row),
            pl.BlockSpec((1, SSM_CH), const2),
            pl.BlockSpec((SSM_CH, SSM_CH), const2),
            pl.BlockSpec((1, SSM_CH), const2),
            pl.BlockSpec((D, D), const2),
            pl.BlockSpec((1, D), const2),
            pl.BlockSpec((1, D), const2),
        ],
        out_specs=[pl.BlockSpec((None, TM, D), row), pl.BlockSpec((None, TM, D), row),
                   pl.BlockSpec((D, TM), lambda b, j: (0, b * nt + j))],
        out_shape=[jax.ShapeDtypeStruct((B, L, D), F32), jax.ShapeDtypeStruct((B, L, D), BF16),
                   jax.ShapeDtypeStruct((D, B * L), BF16)],
        compiler_params=_cparams(("parallel", "parallel")), name="mix",
    )(h, o_conv, zs, yf, yb, o_mla, lw["ssm_d"], lw["ssm_glu_w"], lw["ssm_glu_b"], lw["w_out"],
      lw["ln1_g"], lw["ln1_b"])


def _top_desc(s, n):
    m = jnp.max(s, axis=0, keepdims=True)
    out = [m]
    for _ in range(n - 1):
        m = jnp.max(jnp.where(s < m, s, -jnp.inf), axis=0, keepdims=True)
        out.append(m)
    return out


def _oddeven_merge_sort_pairs(n):
    pairs, p = [], 1
    while p < n:
        k = p
        while k >= 1:
            for j in range(k % p, n - k, 2 * k):
                for i in range(min(k, n - j - k)):
                    if (i + j) // (2 * p) == (i + j + k) // (2 * p):
                        pairs.append((i + j, i + j + k))
            k //= 2
        p *= 2
    return tuple(pairs)


def _top_desc_keys(s, n):
    groups = s.shape[0] // SUBLANE
    r = [s[SUBLANE * i:SUBLANE * (i + 1), :] for i in range(groups)]
    for a, b in _oddeven_merge_sort_pairs(groups):
        r[a], r[b] = jnp.maximum(r[a], r[b]), jnp.minimum(r[a], r[b])
    ninf = jnp.full(r[0].shape, -jnp.inf, F32)
    out = []
    for k in range(n):
        m = jnp.max(r[0], axis=0, keepdims=True)
        out.append(m)
        if k == n - 1:
            break
        won = r[0] == m
        for i in range(min(groups, n - 1 - k)):
            r[i] = jnp.where(won, r[i + 1] if i + 1 < groups else ninf, r[i])
    return out


def _split_bf16(x):
    hi = x.astype(BF16)
    return hi, (x - hi.astype(F32)).astype(BF16)


def _route_kernel(x_ref, wq_ref, k1_ref, k2_ref, ns1_ref, e1_ref, s2m_ref, e2_ref, cand_ref):
    q = _dot(x_ref[...], wq_ref[...])
    k_hl = [_split_bf16(k1_ref[...]), _split_bf16(k2_ref[...])]
    cand_ref[...] = jnp.full(cand_ref.shape, -jnp.inf, F32)
    for h in range(PEER_HEADS):
        s, tops = [], []
        for half in range(2):
            lo = h * 2 * PEER_HALF + half * PEER_HALF
            q_hi, q_lo = _split_bf16(q[:, lo:lo + PEER_HALF])
            k_hi, k_lo = k_hl[half]
            sc = _dot_nt(k_hi, q_hi) + (_dot_nt(k_hi, q_lo) + _dot_nt(k_lo, q_hi))
            s.append(sc)
            tops.append(_top_desc_keys(sc, _CAND_N))
        for r, (a, b) in enumerate(_CAND_PAIRS):
            cand_ref[r:r + 1, :] = tops[0][a] + tops[1][b]
        cs = _top_desc(cand_ref[...], _CAND_N)
        thr = 0.5 * (cs[PEER_TOPK - 1] + cs[PEER_TOPK])
        z = jnp.ones_like(thr)
        for kk in range(1, PEER_TOPK):
            z = z + jnp.exp(cs[kk] - cs[0])
        outs = ((ns1_ref, -s[0]), (e1_ref, jnp.exp(s[0] - tops[0][0]) / z),
                (s2m_ref, s[1] - thr), (e2_ref, jnp.exp(s[1] - tops[1][0])))
        for ref, val in outs:
            for c in range(val.shape[1] // LANE):
                ref[h, c] = val[:, c * LANE:(c + 1) * LANE]


def _route(xb, lw):
    T, D = xb.shape
    TT = min(256, T)
    aux = jax.ShapeDtypeStruct((PEER_HEADS, T // LANE, PEER_NKEYS, LANE), F32)
    aux_spec = pl.BlockSpec((PEER_HEADS, TT // LANE, PEER_NKEYS, LANE), lambda i: (0, i, 0, 0))
    return pl.pallas_call(
        _route_kernel,
        grid=(T // TT,),
        in_specs=[
            pl.BlockSpec((TT, D), lambda i: (i, 0)),
            pl.BlockSpec((D, 2 * PEER_HEADS * PEER_HALF), lambda i: (0, 0)),
            pl.BlockSpec((PEER_NKEYS, PEER_HALF), lambda i: (0, 0)),
            pl.BlockSpec((PEER_NKEYS, PEER_HALF), lambda i: (0, 0)),
        ],
        out_specs=[aux_spec] * 4,
        out_shape=[aux] * 4,
        scratch_shapes=[pltpu.VMEM((_CAND_ROWS, TT), F32)],
        compiler_params=_cparams(("parallel",)), name="route",
    )(xb, lw["peer_w_q"], lw["peer_k1"], lw["peer_k2"])


MXU_TILE = 256


def _peer_kernel(xt_ref, u_ref, vt_ref, ns1_ref, e1_ref, s2m_ref, e2_ref, o_ref, act_ref, ag_ref, acc_ref,
                 *, ne, eb, tt):
    e = pl.program_id(1)

    @pl.when(e == 0)
    def _():
        acc_ref[...] = jnp.zeros(acc_ref.shape, F32)

    n1 = eb // PEER_NKEYS
    rows = pl.ds(pl.multiple_of(e * n1, n1), n1)
    per_tile = MXU_TILE // LANE
    for tn in range(tt // MXU_TILE):
        tcols = pl.ds(tn * MXU_TILE, MXU_TILE)
        for tk in range(eb // MXU_TILE):
            trows = pl.ds(tk * MXU_TILE, MXU_TILE)
            act_ref[trows, tcols] = _dot(u_ref[trows, :], xt_ref[:, tcols])
            for c in range(tn * per_tile, (tn + 1) * per_tile):
                cols = pl.ds(c * LANE, LANE)
                ns = [ns1_ref[h, c, rows, :] for h in range(PEER_HEADS)]
                ee = [e1_ref[h, c, rows, :] for h in range(PEER_HEADS)]
                for i in range(tk * per_tile, (tk + 1) * per_tile):
                    gate = jnp.zeros((PEER_NKEYS, LANE), F32)
                    for h in range(PEER_HEADS):
                        sel = s2m_ref[h, c] > ns[h][i:i + 1, :]
                        gate = gate + jnp.where(sel, e2_ref[h, c] * ee[h][i:i + 1, :], 0.0)
                    er = pl.ds(i * PEER_NKEYS, PEER_NKEYS)
                    ag_ref[er, cols] = (_gelu(act_ref[er, cols]) * gate).astype(BF16)
    acc_ref[...] += _dot(vt_ref[...], ag_ref[...])

    @pl.when(e == ne - 1)
    def _():
        o_ref[...] = acc_ref[...].T


def _peer(xt, aux, lw):
    D, T = xt.shape
    TT = min(512, T)
    EB = 2048
    ne = PEER_EXPERTS // EB
    aux_spec = pl.BlockSpec((PEER_HEADS, TT // LANE, PEER_NKEYS, LANE), lambda i, e: (0, i, 0, 0))
    return pl.pallas_call(
        functools.partial(_peer_kernel, ne=ne, eb=EB, tt=TT),
        grid=(T // TT, ne),
        in_specs=[
            pl.BlockSpec((D, TT), lambda i, e: (0, i)),
            pl.BlockSpec((EB, D), lambda i, e: (e, 0)),
            pl.BlockSpec((D, EB), lambda i, e: (0, e)),
        ] + [aux_spec] * 4,
        out_specs=pl.BlockSpec((TT, D), lambda i, e: (i, 0)),
        out_shape=jax.ShapeDtypeStruct((T, D), F32),
        scratch_shapes=[pltpu.VMEM((EB, TT), F32), pltpu.VMEM((EB, TT), BF16), pltpu.VMEM((D, TT), F32)],
        compiler_params=_cparams(("parallel", "arbitrary")), name="peer",
    )(xt, lw["peer_u"], lw["peer_vt"], *aux)


def _ple_kernel(h1_ref, h1b_ref, pe_ref, p_ref, wg_ref, wp_ref, g_ref, b_ref, o_ref):
    gate = _sigmoid(_dot(h1b_ref[...], wg_ref[...]))
    emb = _dot(p_ref[...].astype(BF16), wp_ref[...])
    r = DEEPNORM_ALPHA * h1_ref[...] + pe_ref[...] + gate * emb
    o_ref[...] = _layer_norm(r, g_ref[...], b_ref[...])


def _ple(h1, h1b, peer_out, p, lw):
    T, D = h1.shape
    TM = min(512, T)
    row = lambda i: (i, 0)
    const = lambda i: (0, 0)
    return pl.pallas_call(
        _ple_kernel,
        grid=(T // TM,),
        in_specs=[
            pl.BlockSpec((TM, D), row), pl.BlockSpec((TM, D), row), pl.BlockSpec((TM, D), row),
            pl.BlockSpec((TM, PLE_DIM), row),
            pl.BlockSpec((D, D), const), pl.BlockSpec((PLE_DIM, D), const),
            pl.BlockSpec((1, D), const), pl.BlockSpec((1, D), const),
        ],
        out_specs=pl.BlockSpec((TM, D), row),
        out_shape=jax.ShapeDtypeStruct((T, D), F32),
        compiler_params=_cparams(("parallel",)), name="ple",
    )(h1, h1b, peer_out, p, lw["ple_w_g"], lw["ple_w_p"], lw["ln2_g"], lw["ln2_b"])


def _ssm_params(a_re, a_im, b_re, b_im, c_re, c_im, log_dt):
    G, P, C = SSM_GROUPS, SSM_STATE, SSM_GROUP
    ar, ai = a_re.astype(F32), a_im.astype(F32)
    dt = jnp.exp(log_dt.astype(F32))[..., None]
    mag = jnp.exp(dt * ar)
    abr, abi = mag * jnp.cos(dt * ai), mag * jnp.sin(dt * ai)
    den = ar * ar + ai * ai
    qr = ((abr - 1.0) * ar + abi * ai) / den
    qi = (abi * ar - (abr - 1.0) * ai) / den
    br, bi = b_re.astype(F32), b_im.astype(F32)
    bbr = qr[..., None] * br - qi[..., None] * bi
    bbi = qr[..., None] * bi + qi[..., None] * br
    eye = jnp.eye(G, dtype=F32)

    def bdiag_in(m):
        return jnp.einsum("dgpc,gh->dgchp", m, eye).reshape(2, G * C, G * P)

    def bdiag_out(m):
        return jnp.einsum("dgcp,gh->dgphc", m, eye).reshape(2, G * P, G * C)

    bm = jnp.concatenate([bdiag_in(bbr), bdiag_in(bbi)], axis=2).astype(BF16)
    cm = jnp.concatenate([bdiag_out(c_re.astype(F32)), -bdiag_out(c_im.astype(F32))], axis=1).astype(BF16)
    a1 = (abr.reshape(2, G * P), abi.reshape(2, G * P))
    a2 = (a1[0] * a1[0] - a1[1] * a1[1], 2.0 * a1[0] * a1[1])
    zero = (jnp.zeros_like(a1[0]), jnp.zeros_like(a1[0]))
    pick = lambda z, d: (z[0][d:d + 1], z[1][d:d + 1])

    def rows(top, bottom):
        shape = (top[0].shape[0], SUBLANE // 2, G * P)
        return [jnp.concatenate([jnp.broadcast_to(t[:, None, :], shape),
                                 jnp.broadcast_to(b[:, None, :], shape)], axis=1) for t, b in zip(top, bottom)]

    c8 = jnp.stack([*rows(a1, a1), *rows(zero, zero)], axis=1)
    f1, f2 = rows(pick(a1, 0), pick(a2, 0)), rows(pick(zero, 0), pick(a1, 0))
    b1, b2 = rows(pick(a2, 1), pick(a1, 1)), rows(pick(a1, 1), pick(zero, 1))
    c4 = jnp.concatenate([jnp.stack([*f1, *f2], axis=1), jnp.stack([*b1, *b2], axis=1)], axis=0)
    return bm, cm, c8.astype(F32), c4.astype(F32)


def _rope_tables(L):
    pos = jnp.arange(L, dtype=F32)
    inv_freq = ROPE_THETA ** (-jnp.arange(0, MLA_ROPE, 2, dtype=F32) / MLA_ROPE)
    ang = pos[:, None] * inv_freq[None, :]
    cos, sin = jnp.cos(ang), jnp.sin(ang)
    z = jnp.zeros_like(cos)
    return (jnp.concatenate([cos, cos, z, z], axis=1),
            jnp.concatenate([-sin, z, z, z], axis=1),
            jnp.concatenate([z, sin, z, z], axis=1))


def _layer_weights(i, w_in, conv_w, conv_b, conv_ln_g, conv_ln_b,
                   ssm_a_re, ssm_a_im, ssm_b_re, ssm_b_im, ssm_c_re, ssm_c_im, ssm_log_dt, ssm_d, ssm_glu_w,
                   ssm_glu_b, mla_q_norm_g, mla_w_uq, mla_kv_norm_g, mla_w_ukv, w_out, ln1_g, ln1_b,
                   peer_w_q, peer_k1, peer_k2, peer_u, peer_v, ple_w_p, ple_w_g, ln2_g, ln2_b):
    row = lambda v: v[i].reshape(1, -1).astype(F32)
    H = MLA_HEADS
    wuq = mla_w_uq[i].reshape(MLA_Q_RANK, H, MLA_DK)
    wuq = jnp.pad(wuq, ((0, 0), (0, 0), (0, MLA_DKP - MLA_DK))).reshape(MLA_Q_RANK, H * MLA_DKP)
    wukv = mla_w_ukv[i].reshape(MLA_KV_RANK, H, MLA_NOPE + MLA_V)
    wukv = jnp.concatenate([wukv[:, :, :MLA_NOPE].reshape(MLA_KV_RANK, H * MLA_NOPE),
                            wukv[:, :, MLA_NOPE:].reshape(MLA_KV_RANK, H * MLA_V)], axis=1)
    bm, cm, c8, c4 = _ssm_params(ssm_a_re[i], ssm_a_im[i], ssm_b_re[i], ssm_b_im[i], ssm_c_re[i], ssm_c_im[i],
                                 ssm_log_dt[i])
    return {
        "w_in": jnp.pad(w_in[i], ((0, 0), (0, IN_PAD - w_in.shape[2]))).astype(BF16),
        "q_norm_g": row(mla_q_norm_g), "w_uq": wuq.astype(BF16),
        "kv_norm_g": row(mla_kv_norm_g), "w_ukv": wukv.astype(BF16),
        "conv_w": conv_w[i].astype(F32), "conv_b": row(conv_b),
        "conv_ln_g": row(conv_ln_g), "conv_ln_b": row(conv_ln_b),
        "ssm_bm": bm, "ssm_cm": cm, "ssm_coef8": c8, "ssm_coef4": c4,
        "ssm_d": row(ssm_d), "ssm_glu_w": ssm_glu_w[i].astype(BF16), "ssm_glu_b": row(ssm_glu_b),
        "w_out": w_out[i].astype(BF16), "ln1_g": row(ln1_g), "ln1_b": row(ln1_b),
        "peer_w_q": peer_w_q[i].astype(BF16), "peer_k1": peer_k1[i].astype(F32), "peer_k2": peer_k2[i].astype(F32),
        "peer_u": peer_u[i].astype(BF16), "peer_vt": peer_v[i].T.astype(BF16),
        "ple_w_p": ple_w_p[i].astype(BF16), "ple_w_g": ple_w_g[i].astype(BF16),
        "ln2_g": row(ln2_g), "ln2_b": row(ln2_b),
    }


def _run(x, p, layers, ln_emb):
    B, L, D = x.shape
    assert B in (SUBLANE // 2, SUBLANE), "the S5 scan packs 4 or 8 batch rows per sublane group"
    rope_tabs = _rope_tables(L)
    h = x
    for i, lw in enumerate(layers):
        if i == 0:
            h, zc, zs, q, k, v = _in_proj(h, lw, rope_tabs, ln=ln_emb)
        else:
            zc, zs, q, k, v = _in_proj(h, lw, rope_tabs)
        o_mla = _attention(q, k, v)
        o_conv = _conv(zc, lw)
        yf, yb = _ssm(zs.reshape(L * B, SSM_CH), lw, B)
        h1, h1b, h1t = _mix(h, o_conv, zs, yf.reshape(L, B * SSM_CH), yb.reshape(L, B * SSM_CH), o_mla, lw)
        h1 = h1.reshape(B * L, D)
        h1b = h1b.reshape(B * L, D)
        aux = _route(h1b, lw)
        peer_out = _peer(h1t, aux, lw)
        h = _ple(h1, h1b, peer_out, p[i].reshape(B * L, PLE_DIM), lw).reshape(B, L, D)
    return h


def kernel(x_prompt, x_sample, p_prompt, p_sample, ln_emb_g, ln_emb_b, w_in, conv_w, conv_b, conv_ln_g, conv_ln_b, ssm_a_re, ssm_a_im, ssm_b_re, ssm_b_im, ssm_c_re, ssm_c_im, ssm_log_dt, ssm_d, ssm_glu_w, ssm_glu_b, mla_q_norm_g, mla_w_uq, mla_kv_norm_g, mla_w_ukv, w_out, ln1_g, ln1_b, peer_w_q, peer_k1, peer_k2, peer_u, peer_v, ple_w_p, ple_w_g, ln2_g, ln2_b):
    layers = [
        _layer_weights(i, w_in, conv_w, conv_b, conv_ln_g, conv_ln_b,
                       ssm_a_re, ssm_a_im, ssm_b_re, ssm_b_im, ssm_c_re, ssm_c_im, ssm_log_dt, ssm_d, ssm_glu_w,
                       ssm_glu_b, mla_q_norm_g, mla_w_uq, mla_kv_norm_g, mla_w_ukv, w_out, ln1_g, ln1_b,
                       peer_w_q, peer_k1, peer_k2, peer_u, peer_v, ple_w_p, ple_w_g, ln2_g, ln2_b)
        for i in range(DEPTH)
    ]
    ln_emb = (ln_emb_g.reshape(1, -1).astype(F32), ln_emb_b.reshape(1, -1).astype(F32))
    y_prompt = _run(x_prompt, p_prompt, layers, ln_emb)
    y_sample = _run(x_sample, p_sample, layers, ln_emb)
    return (y_prompt, y_sample)
```

```python
import functools
import math

import jax
import jax.numpy as jnp
import numpy as np
from jax import lax
from jax.experimental import pallas as pl
from jax.experimental.pallas import tpu as pltpu

F32 = jnp.float32
BF16 = jnp.bfloat16

D_MODEL = 1024
DEPTH = 2
CONV_CH = 256
CONV_WIDTH = 31
CONV_HALO = 16
SSM_CH = 256
SSM_GROUP = 16
SSM_GROUPS = 16
SSM_STATE = 64
SSM_LANES = SSM_GROUPS * SSM_STATE
MLA_HEADS = 4
MLA_NOPE = 128
MLA_ROPE = 64
MLA_V = 128
MLA_DK = MLA_NOPE + MLA_ROPE
MLA_DKP = 256
MLA_Q_RANK = 512
MLA_KV_RANK = 256
ROPE_THETA = 10000.0
PEER_HEADS = 8
PEER_NKEYS = 128
PEER_EXPERTS = PEER_NKEYS * PEER_NKEYS
PEER_HALF = 128
PEER_TOPK = 16
PLE_DIM = 256
DEEPNORM_ALPHA = float((2 * DEPTH) ** 0.25)
LN_EPS = 1e-5
RMS_EPS = 1e-6
IN_PAD = 1664

LANE = 128
SUBLANE = 8
BF16_ROWS = 16
PACKED_KEYS = 64
VMEM_LIMIT = 56 * 1024 * 1024

_CAND_N = PEER_TOPK + 1
_CAND_PAIRS = tuple((i, j) for i in range(_CAND_N) for j in range(_CAND_N) if (i + 1) * (j + 1) <= _CAND_N)
_CAND_ROWS = -(-len(_CAND_PAIRS) // SUBLANE) * SUBLANE


def _cparams(sem, flags=None):
    return pltpu.CompilerParams(dimension_semantics=sem, vmem_limit_bytes=VMEM_LIMIT, flags=flags)


def _sigmoid(x):
    return 1.0 / (1.0 + jnp.exp(-x))


def _gelu(x):
    return 0.5 * x * (1.0 + jnp.tanh(math.sqrt(2.0 / math.pi) * (x + 0.044715 * (x * x * x))))


def _layer_norm(x, g, b):
    mu = jnp.mean(x, axis=-1, keepdims=True)
    xc = x - mu
    var = jnp.mean(xc * xc, axis=-1, keepdims=True)
    return xc * lax.rsqrt(var + LN_EPS) * g + b


def _rms_norm(x, g):
    return x * lax.rsqrt(jnp.mean(x * x, axis=-1, keepdims=True) + RMS_EPS) * g


def _dot(a, b):
    return jnp.dot(a, b, preferred_element_type=F32)


def _dot_nt(a, b):
    return lax.dot_general(a, b, (((1,), (1,)), ((), ())), preferred_element_type=F32)


def _in_proj_kernel(*refs, pre_ln):
    if pre_ln:
        (x_ref, g_ref, b_ref, w_ref, qg_ref, wuq_ref, kvg_ref, wukv_ref, rc_ref, rs1_ref, rs2_ref,
         h_ref, zc_ref, zs_ref, q_ref, k_ref, v_ref) = refs
    else:
        (x_ref, w_ref, qg_ref, wuq_ref, kvg_ref, wukv_ref, rc_ref, rs1_ref, rs2_ref,
         zc_ref, zs_ref, q_ref, k_ref, v_ref) = refs
    x = x_ref[...]
    if pre_ln:
        x = _layer_norm(x, g_ref[...], b_ref[...])
        h_ref[...] = x
    z = _dot(x.astype(BF16), w_ref[...])
    zc_ref[...] = z[:, 0:512]
    zs_ref[...] = z[:, 512:768]
    cq = _rms_norm(z[:, 768:1280], qg_ref[...])
    ckv = _rms_norm(z[:, 1280:1536], kvg_ref[...])
    kr = z[:, 1536:1664]
    q = _dot(cq.astype(BF16), wuq_ref[...])
    kv = _dot(ckv.astype(BF16), wukv_ref[...])
    rc, rs1, rs2 = rc_ref[...], rs1_ref[...], rs2_ref[...]

    def rope(t):
        return t * rc + pltpu.roll(t, 96, 1) * rs1 + pltpu.roll(t, 32, 1) * rs2

    scale = MLA_DK ** -0.5
    k_rope = rope(kr).astype(BF16)
    for h in range(MLA_HEADS):
        base = h * MLA_DKP
        q_ref[h, :, 0:128] = (q[:, base:base + 128] * scale).astype(BF16)
        q_ref[h, :, 128:256] = (rope(q[:, base + 128:base + 256]) * scale).astype(BF16)
        k_ref[h, :, 0:128] = kv[:, h * 128:(h + 1) * 128].astype(BF16)
        k_ref[h, :, 128:256] = k_rope
        v_ref[h] = kv[:, 512 + h * 128:512 + (h + 1) * 128].astype(BF16)


def _in_proj(x, lw, rope_tabs, ln=None):
    B, L, D = x.shape
    TM = min(512, L)
    nt = L // TM
    pre_ln = ln is not None
    row = lambda b, j: (b, j, 0)
    const2 = lambda b, j: (0, 0)
    in_specs = [pl.BlockSpec((None, TM, D), row)]
    args = [x]
    if pre_ln:
        in_specs += [pl.BlockSpec((1, D), const2)] * 2
        args += [ln[0], ln[1]]
    in_specs += [
        pl.BlockSpec((D, IN_PAD), const2),
        pl.BlockSpec((1, MLA_Q_RANK), const2),
        pl.BlockSpec((MLA_Q_RANK, MLA_HEADS * MLA_DKP), const2),
        pl.BlockSpec((1, MLA_KV_RANK), const2),
        pl.BlockSpec((MLA_KV_RANK, 2 * MLA_HEADS * 128), const2),
    ] + [pl.BlockSpec((TM, LANE), lambda b, j: (j, 0))] * 3
    args += [lw["w_in"], lw["q_norm_g"], lw["w_uq"], lw["kv_norm_g"], lw["w_ukv"], *rope_tabs]
    out_shape, out_specs = [], []
    if pre_ln:
        out_shape.append(jax.ShapeDtypeStruct((B, L, D), F32))
        out_specs.append(pl.BlockSpec((None, TM, D), row))
    out_shape += [
        jax.ShapeDtypeStruct((B, L, 2 * CONV_CH), F32),
        jax.ShapeDtypeStruct((L, B * SSM_CH), F32),
        jax.ShapeDtypeStruct((B, MLA_HEADS, L, MLA_DKP), BF16),
        jax.ShapeDtypeStruct((B, MLA_HEADS, L, MLA_DKP), BF16),
        jax.ShapeDtypeStruct((B, MLA_HEADS, L, MLA_V), BF16),
    ]
    hrow = lambda b, j: (b, 0, j, 0)
    out_specs += [
        pl.BlockSpec((None, TM, 2 * CONV_CH), row),
        pl.BlockSpec((TM, SSM_CH), lambda b, j: (j, b)),
        pl.BlockSpec((None, MLA_HEADS, TM, MLA_DKP), hrow),
        pl.BlockSpec((None, MLA_HEADS, TM, MLA_DKP), hrow),
        pl.BlockSpec((None, MLA_HEADS, TM, MLA_V), hrow),
    ]
    return pl.pallas_call(
        functools.partial(_in_proj_kernel, pre_ln=pre_ln),
        grid=(B, nt), in_specs=in_specs, out_specs=out_specs, out_shape=out_shape,
        compiler_params=_cparams(("parallel", "parallel")), name="in_proj",
    )(*args)


def _attn_kernel(q_ref, k_ref, v_ref, o_ref, m_ref, l_ref, acc_ref, s_ref, p_ref, *, nk, tk):
    j = pl.program_id(3)

    @pl.when(j == 0)
    def _():
        m_ref[...] = jnp.full(m_ref.shape, -jnp.inf, F32)
        l_ref[...] = jnp.zeros(l_ref.shape, F32)
        acc_ref[...] = jnp.zeros(acc_ref.shape, F32)

    s_ref[...] = _dot_nt(q_ref[...], k_ref[...])
    chunks = [pl.ds(c * LANE, LANE) for c in range(tk // LANE)]
    m_part = s_ref[:, chunks[0]]
    for ch in chunks[1:]:
        m_part = jnp.maximum(m_part, s_ref[:, ch])
    m_prev = m_ref[...]
    m_new = jnp.maximum(m_prev, jnp.max(m_part, axis=1, keepdims=True))
    alpha = jnp.exp(m_prev - m_new)
    l_part = jnp.zeros(m_new.shape, F32)
    for ch in chunks:
        p = jnp.exp(s_ref[:, ch] - m_new)
        p_ref[:, ch] = p.astype(BF16)
        l_part = l_part + p
    l_ref[...] = alpha * l_ref[...] + jnp.sum(l_part, axis=1, keepdims=True)
    acc_ref[...] = alpha * acc_ref[...] + _dot(p_ref[...], v_ref[...])
    m_ref[...] = m_new

    @pl.when(j == nk - 1)
    def _():
        o_ref[...] = (acc_ref[...] / l_ref[...]).astype(o_ref.dtype)


def _attention(q, k, v):
    B, H, L, _ = q.shape
    TQ = min(1024, L)
    TK = min(2048, L)
    nq, nk = L // TQ, L // TK
    return pl.pallas_call(
        functools.partial(_attn_kernel, nk=nk, tk=TK),
        grid=(B, H, nq, nk),
        in_specs=[
            pl.BlockSpec((None, None, TQ, MLA_DKP), lambda b, h, i, j: (b, h, i, 0)),
            pl.BlockSpec((None, None, TK, MLA_DKP), lambda b, h, i, j: (b, h, j, 0)),
            pl.BlockSpec((None, None, TK, MLA_V), lambda b, h, i, j: (b, h, j, 0)),
        ],
        out_specs=pl.BlockSpec((None, TQ, MLA_V), lambda b, h, i, j: (b, i, h)),
        out_shape=jax.ShapeDtypeStruct((B, L, H * MLA_V), BF16),
        scratch_shapes=[pltpu.VMEM((TQ, LANE), F32), pltpu.VMEM((TQ, LANE), F32),
                        pltpu.VMEM((TQ, MLA_V), F32), pltpu.VMEM((TQ, TK), F32), pltpu.VMEM((TQ, TK), BF16)],
        compiler_params=_cparams(("parallel", "parallel", "parallel", "arbitrary")), name="attn",
    )(q, k, v)


def _conv_kernel(zm_ref, zp_ref, zn_ref, w_ref, cb_ref, g_ref, b_ref, o_ref, hp_ref, sh_ref, *, tc, nt, rc):
    j = pl.program_id(1)

    def glu(z):
        return z[:, :CONV_CH] * _sigmoid(z[:, CONV_CH:])

    hp_ref[0:CONV_HALO, :] = jnp.where(j > 0, glu(zp_ref[...]), 0.0)
    hp_ref[CONV_HALO:CONV_HALO + tc, :] = glu(zm_ref[...])
    hp_ref[CONV_HALO + tc:2 * CONV_HALO + tc, :] = jnp.where(j < nt - 1, glu(zn_ref[...]), 0.0)
    shift = CONV_HALO - CONV_WIDTH // 2
    span = -(-(CONV_WIDTH + shift) // SUBLANE) * SUBLANE
    for a in range(SUBLANE):
        sh_ref[a] = hp_ref[a:a + tc + span - SUBLANE, :]
    for r in range(tc // rc):
        acc = jnp.zeros((rc, CONV_CH), F32)
        for a in range(SUBLANE):
            for mm in range(span // SUBLANE):
                kk = mm * SUBLANE + a - shift
                if 0 <= kk < CONV_WIDTH:
                    lo = r * rc + mm * SUBLANE
                    acc = acc + sh_ref[a, lo:lo + rc, :] * w_ref[kk:kk + 1, :]
        y = _layer_norm(acc + cb_ref[...], g_ref[...], b_ref[...])
        o_ref[r * rc:(r + 1) * rc, :] = (y * _sigmoid(y)).astype(o_ref.dtype)


def _conv(zc, lw):
    B, L, _ = zc.shape
    TC = min(512, L)
    nt = L // TC
    hb = TC // CONV_HALO
    nhb = L // CONV_HALO
    const2 = lambda b, j: (0, 0)
    return pl.pallas_call(
        functools.partial(_conv_kernel, tc=TC, nt=nt, rc=min(64, TC)),
        grid=(B, nt),
        in_specs=[
            pl.BlockSpec((None, TC, 2 * CONV_CH), lambda b, j: (b, j, 0)),
            pl.BlockSpec((None, CONV_HALO, 2 * CONV_CH), lambda b, j: (b, jnp.maximum(j * hb - 1, 0), 0)),
            pl.BlockSpec((None, CONV_HALO, 2 * CONV_CH), lambda b, j: (b, jnp.minimum((j + 1) * hb, nhb - 1), 0)),
            pl.BlockSpec((CONV_WIDTH, CONV_CH), const2),
            pl.BlockSpec((1, CONV_CH), const2),
            pl.BlockSpec((1, CONV_CH), const2),
            pl.BlockSpec((1, CONV_CH), const2),
        ],
        out_specs=pl.BlockSpec((None, TC, CONV_CH), lambda b, j: (b, j, 0)),
        out_shape=jax.ShapeDtypeStruct((B, L, CONV_CH), BF16),
        scratch_shapes=[pltpu.VMEM((TC + 2 * CONV_HALO, CONV_CH), F32),
                        pltpu.VMEM((SUBLANE, TC + 2 * CONV_HALO - SUBLANE, CONV_CH), F32)],
        compiler_params=_cparams(("parallel", "parallel")), name="conv",
    )(zc, zc, zc, lw["conv_w"], lw["conv_b"], lw["conv_ln_g"], lw["conv_ln_b"])


def _ssm_kernel(uf_ref, ub_ref, bm_ref, coef_ref, cm_ref, yf_ref, yb_ref,
                sf_ref, sb_ref, hf_ref, hb_ref, *, nb, steps, lc):
    i = pl.program_id(0)
    S = SSM_LANES

    @pl.when(i == 0)
    def _():
        hf_ref[...] = jnp.zeros(hf_ref.shape, F32)
        hb_ref[...] = jnp.zeros(hb_ref.shape, F32)

    sf_ref[...] = _dot(uf_ref[...].astype(BF16), bm_ref[0])
    sb_ref[...] = _dot(ub_ref[...].astype(BF16), bm_ref[1])

    lo_half = lax.broadcasted_iota(jnp.int32, (SUBLANE, lc), 0) < nb

    for c0 in range(0, S, lc):
        re_l = pl.ds(c0, lc)
        im_l = pl.ds(S + c0, lc)
        cf = [coef_ref[0, k, :, c0:c0 + lc] for k in range(4)]
        cb = [coef_ref[1, k, :, c0:c0 + lc] for k in range(4)]

        def body(t, carry, re_l=re_l, im_l=im_l, cf=cf, cb=cb):
            hfr, hfi, hbr, hbi = carry
            rf = pl.ds(pl.multiple_of(t * SUBLANE, SUBLANE), SUBLANE)
            rb = pl.ds(pl.multiple_of((steps - 1 - t) * SUBLANE, SUBLANE), SUBLANE)
            out = []
            for ref, rows, c, hr, hi, fwd in ((sf_ref, rf, cf, hfr, hfi, True), (sb_ref, rb, cb, hbr, hbi, False)):
                xr = ref[rows, re_l]
                xi = ref[rows, im_l]
                nr = c[0] * hr - c[1] * hi + xr
                ni = c[0] * hi + c[1] * hr + xi
                if nb != SUBLANE:
                    sxr = pltpu.roll(xr, nb, 0)
                    sxi = pltpu.roll(xi, nb, 0)
                    nr = nr + (c[2] * sxr - c[3] * sxi)
                    ni = ni + (c[2] * sxi + c[3] * sxr)
                ref[rows, re_l] = nr
                ref[rows, im_l] = ni
                if nb != SUBLANE:
                    snr = pltpu.roll(nr, nb, 0)
                    sni = pltpu.roll(ni, nb, 0)
                    if fwd:
                        nr = jnp.where(lo_half, snr, nr)
                        ni = jnp.where(lo_half, sni, ni)
                    else:
                        nr = jnp.where(lo_half, nr, snr)
                        ni = jnp.where(lo_half, ni, sni)
                out += [nr, ni]
            return tuple(out)

        init = (hf_ref[:, re_l], hf_ref[:, im_l], hb_ref[:, re_l], hb_ref[:, im_l])
        hfr, hfi, hbr, hbi = lax.fori_loop(0, steps, body, init, unroll=2)
        hf_ref[:, re_l] = hfr
        hf_ref[:, im_l] = hfi
        hb_ref[:, re_l] = hbr
        hb_ref[:, im_l] = hbi

    yf_ref[...] = _dot(sf_ref[...].astype(BF16), cm_ref[0])
    yb_ref[...] = _dot(sb_ref[...].astype(BF16), cm_ref[1])


def _ssm(u_rows, lw, nb):
    N = u_rows.shape[0]
    R = min(1024, N)
    n = N // R
    steps = R // SUBLANE
    S2 = 2 * SSM_LANES
    return pl.pallas_call(
        functools.partial(_ssm_kernel, nb=nb, steps=steps, lc=512),
        grid=(n,),
        in_specs=[
            pl.BlockSpec((R, SSM_CH), lambda i: (i, 0)),
            pl.BlockSpec((R, SSM_CH), lambda i: (n - 1 - i, 0)),
            pl.BlockSpec((2, SSM_CH, S2), lambda i: (0, 0, 0)),
            pl.BlockSpec((2, 4, SUBLANE, SSM_LANES), lambda i: (0, 0, 0, 0)),
            pl.BlockSpec((2, S2, SSM_CH), lambda i: (0, 0, 0)),
        ],
        out_specs=[
            pl.BlockSpec((R, SSM_CH), lambda i: (i, 0)),
            pl.BlockSpec((R, SSM_CH), lambda i: (n - 1 - i, 0)),
        ],
        out_shape=[jax.ShapeDtypeStruct((N, SSM_CH), F32)] * 2,
        scratch_shapes=[pltpu.VMEM((R, S2), F32), pltpu.VMEM((R, S2), F32),
                        pltpu.VMEM((SUBLANE, S2), F32), pltpu.VMEM((SUBLANE, S2), F32)],
        compiler_params=_cparams(("arbitrary",)), name="ssm",
    )(u_rows, u_rows, lw["ssm_bm"], lw["ssm_coef" + str(nb)], lw["ssm_cm"])


def _mix_kernel(h_ref, oc_ref, u_ref, yf_ref, yb_ref, om_ref, d_ref, gw_ref, gb_ref, wo_ref, g_ref, b_ref,
                h1_ref, h1b_ref, h1t_ref):
    y = d_ref[...] * u_ref[...] + yf_ref[...] + yb_ref[...]
    g = _gelu(y)
    o_ssm = g * _sigmoid(_dot(g.astype(BF16), gw_ref[...]) + gb_ref[...])
    mix = (_dot(oc_ref[...], wo_ref[0:256, :]) + _dot(o_ssm.astype(BF16), wo_ref[256:512, :])
           + _dot(om_ref[...], wo_ref[512:1024, :]))
    h1 = _layer_norm(DEEPNORM_ALPHA * h_ref[...] + mix, g_ref[...], b_ref[...])
    h1_ref[...] = h1
    h1b_ref[...] = h1.astype(BF16)
    h1t_ref[...] = h1.T.astype(BF16)


def _mix(h, o_conv, zs, yf, yb, o_mla, lw):
    B, L, D = h.shape
    TM = min(512, L)
    nt = L // TM
    row = lambda b, j: (b, j, 0)
    tb = lambda b, j: (j, b)
    const2 = lambda b, j: (0, 0)
    return pl.pallas_call(
        _mix_kernel,
        grid=(B, nt),
        in_specs=[
            pl.BlockSpec((None, TM, D), row),
            pl.BlockSpec((None, TM, CONV_CH), row),
            pl.BlockSpec((TM, SSM_CH), tb),
            pl.BlockSpec((TM, SSM_CH), tb),
            pl.BlockSpec((TM, SSM_CH), tb),
            pl.BlockSpec((None, TM, MLA_HEADS * MLA_V), row),
            pl.BlockSpec((1, SSM_CH), const2),
            pl.BlockSpec((SSM_CH, SSM_CH), const2),
            pl.BlockSpec((1, SSM_CH), const2),
            pl.BlockSpec((D, D), const2),
            pl.BlockSpec((1, D), const2),
            pl.BlockSpec((1, D), const2),
        ],
        out_specs=[pl.BlockSpec((None, TM, D), row), pl.BlockSpec((None, TM, D), row),
                   pl.BlockSpec((D, TM), lambda b, j: (0, b * nt + j))],
        out_shape=[jax.ShapeDtypeStruct((B, L, D), F32), jax.ShapeDtypeStruct((B, L, D), BF16),
                   jax.ShapeDtypeStruct((D, B * L), BF16)],
        compiler_params=_cparams(("parallel", "parallel")), name="mix",
    )(h, o_conv, zs, yf, yb, o_mla, lw["ssm_d"], lw["ssm_glu_w"], lw["ssm_glu_b"], lw["w_out"],
      lw["ln1_g"], lw["ln1_b"])


def _top_desc(s, n):
    m = jnp.max(s, axis=0, keepdims=True)
    out = [m]
    for _ in range(n - 1):
        m = jnp.max(jnp.where(s < m, s, -jnp.inf), axis=0, keepdims=True)
        out.append(m)
    return out


def _oddeven_merge_sort_pairs(n):
    pairs, p = [], 1
    while p < n:
        k = p
        while k >= 1:
            for j in range(k % p, n - k, 2 * k):
                for i in range(min(k, n - j - k)):
                    if (i + j) // (2 * p) == (i + j + k) // (2 * p):
                        pairs.append((i + j, i + j + k))
            k //= 2
        p *= 2
    return tuple(pairs)


def _top_desc_keys(s, n):
    groups = s.shape[0] // SUBLANE
    r = [s[SUBLANE * i:SUBLANE * (i + 1), :] for i in range(groups)]
    for a, b in _oddeven_merge_sort_pairs(groups):
        r[a], r[b] = jnp.maximum(r[a], r[b]), jnp.minimum(r[a], r[b])
    ninf = jnp.full(r[0].shape, -jnp.inf, F32)
    out = []
    for k in range(n):
        m = jnp.max(r[0], axis=0, keepdims=True)
        out.append(m)
        if k == n - 1:
            break
        won = r[0] == m
        for i in range(min(groups, n - 1 - k)):
            r[i] = jnp.where(won, r[i + 1] if i + 1 < groups else ninf, r[i])
    return out


def _split_bf16(x):
    hi = x.astype(BF16)
    return hi, (x - hi.astype(F32)).astype(BF16)


def _route_kernel(x_ref, wq_ref, k1_ref, k2_ref, r1_ref, e1_ref, rank2_ref, e2_ref, cand_ref):
    q = _dot(x_ref[...], wq_ref[...])
    k_hl = [_split_bf16(k1_ref[...]), _split_bf16(k2_ref[...])]
    cand_ref[...] = jnp.full(cand_ref.shape, -jnp.inf, F32)
    for h in range(PEER_HEADS):
        s, tops = [], []
        for half in range(2):
            lo = h * 2 * PEER_HALF + half * PEER_HALF
            q_hi, q_lo = _split_bf16(q[:, lo:lo + PEER_HALF])
            k_hi, k_lo = k_hl[half]
            sc = _dot_nt(k_hi, q_hi) + (_dot_nt(k_hi, q_lo) + _dot_nt(k_lo, q_hi))
            s.append(sc)
            tops.append(_top_desc_keys(sc, _CAND_N))
        for r, (a, b) in enumerate(_CAND_PAIRS):
            cand_ref[r:r + 1, :] = tops[0][a] + tops[1][b]
        cs = _top_desc(cand_ref[...], _CAND_N)
        thr = 0.5 * (cs[PEER_TOPK - 1] + cs[PEER_TOPK])
        z = jnp.ones_like(thr)
        for kk in range(1, PEER_TOPK):
            z = z + jnp.exp(cs[kk] - cs[0])
        theta = thr - s[0]
        rank2 = jnp.zeros(s[1].shape, F32)
        r1 = jnp.zeros(s[0].shape, F32)
        for kk in range(PEER_TOPK):
            rank2 = rank2 + jnp.where(s[1] < tops[1][kk], 1.0, 0.0)
            r1 = r1 + jnp.where(tops[1][kk] > theta, 1.0, 0.0)
        outs = ((r1_ref, r1), (e1_ref, jnp.exp(s[0] - tops[0][0]) / z),
                (rank2_ref, rank2), (e2_ref, jnp.exp(s[1] - tops[1][0])))
        for ref, val in outs:
            for c in range(val.shape[1] // LANE):
                tile = val[:, c * LANE:(c + 1) * LANE]
                if len(ref.shape) == 2:
                    words = pltpu.bitcast(tile.astype(BF16), jnp.uint32)
                    ref[pl.ds((c * PEER_HEADS + h) * PACKED_KEYS, PACKED_KEYS), :] = words
                else:
                    ref[h, c] = tile


def _route(xb, lw):
    T, D = xb.shape
    TT = min(256, T)
    aux_f32 = jax.ShapeDtypeStruct((PEER_HEADS, T // LANE, PEER_NKEYS, LANE), F32)
    aux_bf16 = jax.ShapeDtypeStruct((T // LANE * PEER_HEADS * PACKED_KEYS, LANE), jnp.uint32)
    f32_spec = pl.BlockSpec((PEER_HEADS, TT // LANE, PEER_NKEYS, LANE), lambda i: (0, i, 0, 0))
    bf16_spec = pl.BlockSpec((TT // LANE * PEER_HEADS * PACKED_KEYS, LANE), lambda i: (i, 0))
    return pl.pallas_call(
        _route_kernel,
        grid=(T // TT,),
        in_specs=[
            pl.BlockSpec((TT, D), lambda i: (i, 0)),
            pl.BlockSpec((D, 2 * PEER_HEADS * PEER_HALF), lambda i: (0, 0)),
            pl.BlockSpec((PEER_NKEYS, PEER_HALF), lambda i: (0, 0)),
            pl.BlockSpec((PEER_NKEYS, PEER_HALF), lambda i: (0, 0)),
        ],
        out_specs=[f32_spec, f32_spec, bf16_spec, bf16_spec],
        out_shape=[aux_f32, aux_f32, aux_bf16, aux_bf16],
        scratch_shapes=[pltpu.VMEM((_CAND_ROWS, TT), F32)],
        compiler_params=_cparams(("parallel",)), name="route",
    )(xb, lw["peer_w_q"], lw["peer_k1"], lw["peer_k2"])


MXU_TILE = 256


def _peer_kernel(xt_ref, u_ref, vt_ref, r1_ref, e1_ref, rank2_ref, e2_ref, o_ref, act_ref, ag_ref, acc_ref,
                 *, ne, eb, tt):
    e = pl.program_id(1)

    @pl.when(e == 0)
    def _():
        acc_ref[...] = jnp.zeros(acc_ref.shape, F32)

    n1 = eb // PEER_NKEYS
    rows = pl.ds(pl.multiple_of(e * n1, n1), n1)
    per_tile = MXU_TILE // LANE
    for tn in range(tt // MXU_TILE):
        tcols = pl.ds(tn * MXU_TILE, MXU_TILE)
        for tk in range(eb // MXU_TILE):
            trows = pl.ds(tk * MXU_TILE, MXU_TILE)
            act_ref[trows, tcols] = _dot(u_ref[trows, :], xt_ref[:, tcols])
            for c in range(tn * per_tile, (tn + 1) * per_tile):
                cols = pl.ds(c * LANE, LANE)
                rr = [r1_ref[h, c, rows, :] for h in range(PEER_HEADS)]
                ee = [e1_ref[h, c, rows, :] for h in range(PEER_HEADS)]
                for i in range(tk * per_tile, (tk + 1) * per_tile):
                    gate = jnp.zeros((PEER_NKEYS // BF16_ROWS, BF16_ROWS, LANE), BF16)
                    for h in range(PEER_HEADS):
                        r16 = jnp.broadcast_to(rr[h][i:i + 1, :], (BF16_ROWS, LANE)).astype(BF16)
                        e16 = jnp.broadcast_to(ee[h][i:i + 1, :], (BF16_ROWS, LANE)).astype(BF16)
                        tile = pl.ds((c * PEER_HEADS + h) * PACKED_KEYS, PACKED_KEYS)
                        rk = pltpu.bitcast(rank2_ref[tile, :], BF16).reshape(gate.shape)
                        e2 = pltpu.bitcast(e2_ref[tile, :], BF16).reshape(gate.shape)
                        gate = gate + jnp.where(rk < r16[None], e2 * e16[None], jnp.zeros((), BF16))
                    for g in range(PEER_NKEYS // BF16_ROWS):
                        er = pl.ds(i * PEER_NKEYS + g * BF16_ROWS, BF16_ROWS)
                        ag_ref[er, cols] = _gelu(act_ref[er, cols]).astype(BF16) * gate[g]
    acc_ref[...] += _dot(vt_ref[...], ag_ref[...])

    @pl.when(e == ne - 1)
    def _():
        o_ref[...] = acc_ref[...].T


def _peer(xt, aux, lw):
    D, T = xt.shape
    TT = min(512, T)
    EB = 2048
    ne = PEER_EXPERTS // EB
    f32_spec = pl.BlockSpec((PEER_HEADS, TT // LANE, PEER_NKEYS, LANE), lambda i, e: (0, i, 0, 0))
    bf16_spec = pl.BlockSpec((TT // LANE * PEER_HEADS * PACKED_KEYS, LANE), lambda i, e: (i, 0))
    return pl.pallas_call(
        functools.partial(_peer_kernel, ne=ne, eb=EB, tt=TT),
        grid=(T // TT, ne),
        in_specs=[
            pl.BlockSpec((D, TT), lambda i, e: (0, i)),
            pl.BlockSpec((EB, D), lambda i, e: (e, 0)),
            pl.BlockSpec((D, EB), lambda i, e: (0, e)),
            f32_spec, f32_spec, bf16_spec, bf16_spec,
        ],
        out_specs=pl.BlockSpec((TT, D), lambda i, e: (i, 0)),
        out_shape=jax.ShapeDtypeStruct((T, D), F32),
        scratch_shapes=[pltpu.VMEM((EB, TT), F32), pltpu.VMEM((EB, TT), BF16), pltpu.VMEM((D, TT), F32)],
        compiler_params=_cparams(("parallel", "arbitrary")), name="peer",
    )(xt, lw["peer_u"], lw["peer_vt"], *aux)


def _ple_kernel(h1_ref, h1b_ref, pe_ref, p_ref, wg_ref, wp_ref, g_ref, b_ref, o_ref):
    gate = _sigmoid(_dot(h1b_ref[...], wg_ref[...]))
    emb = _dot(p_ref[...].astype(BF16), wp_ref[...])
    r = DEEPNORM_ALPHA * h1_ref[...] + pe_ref[...] + gate * emb
    o_ref[...] = _layer_norm(r, g_ref[...], b_ref[...])


def _ple(h1, h1b, peer_out, p, lw):
    T, D = h1.shape
    TM = min(512, T)
    row = lambda i: (i, 0)
    const = lambda i: (0, 0)
    return pl.pallas_call(
        _ple_kernel,
        grid=(T // TM,),
        in_specs=[
            pl.BlockSpec((TM, D), row), pl.BlockSpec((TM, D), row), pl.BlockSpec((TM, D), row),
            pl.BlockSpec((TM, PLE_DIM), row),
            pl.BlockSpec((D, D), const), pl.BlockSpec((PLE_DIM, D), const),
            pl.BlockSpec((1, D), const), pl.BlockSpec((1, D), const),
        ],
        out_specs=pl.BlockSpec((TM, D), row),
        out_shape=jax.ShapeDtypeStruct((T, D), F32),
        compiler_params=_cparams(("parallel",)), name="ple",
    )(h1, h1b, peer_out, p, lw["ple_w_g"], lw["ple_w_p"], lw["ln2_g"], lw["ln2_b"])


def _ssm_params(a_re, a_im, b_re, b_im, c_re, c_im, log_dt):
    G, P, C = SSM_GROUPS, SSM_STATE, SSM_GROUP
    ar, ai = a_re.astype(F32), a_im.astype(F32)
    dt = jnp.exp(log_dt.astype(F32))[..., None]
    mag = jnp.exp(dt * ar)
    abr, abi = mag * jnp.cos(dt * ai), mag * jnp.sin(dt * ai)
    den = ar * ar + ai * ai
    qr = ((abr - 1.0) * ar + abi * ai) / den
    qi = (abi * ar - (abr - 1.0) * ai) / den
    br, bi = b_re.astype(F32), b_im.astype(F32)
    bbr = qr[..., None] * br - qi[..., None] * bi
    bbi = qr[..., None] * bi + qi[..., None] * br
    eye = jnp.eye(G, dtype=F32)

    def bdiag_in(m):
        return jnp.einsum("dgpc,gh->dgchp", m, eye).reshape(2, G * C, G * P)

    def bdiag_out(m):
        return jnp.einsum("dgcp,gh->dgphc", m, eye).reshape(2, G * P, G * C)

    bm = jnp.concatenate([bdiag_in(bbr), bdiag_in(bbi)], axis=2).astype(BF16)
    cm = jnp.concatenate([bdiag_out(c_re.astype(F32)), -bdiag_out(c_im.astype(F32))], axis=1).astype(BF16)
    a1 = (abr.reshape(2, G * P), abi.reshape(2, G * P))
    a2 = (a1[0] * a1[0] - a1[1] * a1[1], 2.0 * a1[0] * a1[1])
    zero = (jnp.zeros_like(a1[0]), jnp.zeros_like(a1[0]))
    pick = lambda z, d: (z[0][d:d + 1], z[1][d:d + 1])

    def rows(top, bottom):
        shape = (top[0].shape[0], SUBLANE // 2, G * P)
        return [jnp.concatenate([jnp.broadcast_to(t[:, None, :], shape),
                                 jnp.broadcast_to(b[:, None, :], shape)], axis=1) for t, b in zip(top, bottom)]

    c8 = jnp.stack([*rows(a1, a1), *rows(zero, zero)], axis=1)
    f1, f2 = rows(pick(a1, 0), pick(a2, 0)), rows(pick(zero, 0), pick(a1, 0))
    b1, b2 = rows(pick(a2, 1), pick(a1, 1)), rows(pick(a1, 1), pick(zero, 1))
    c4 = jnp.concatenate([jnp.stack([*f1, *f2], axis=1), jnp.stack([*b1, *b2], axis=1)], axis=0)
    return bm, cm, c8.astype(F32), c4.astype(F32)


def _rope_tables(L):
    pos = jnp.arange(L, dtype=F32)
    inv_freq = ROPE_THETA ** (-jnp.arange(0, MLA_ROPE, 2, dtype=F32) / MLA_ROPE)
    ang = pos[:, None] * inv_freq[None, :]
    cos, sin = jnp.cos(ang), jnp.sin(ang)
    z = jnp.zeros_like(cos)
    return (jnp.concatenate([cos, cos, z, z], axis=1),
            jnp.concatenate([-sin, z, z, z], axis=1),
            jnp.concatenate([z, sin, z, z], axis=1))


def _layer_weights(i, w_in, conv_w, conv_b, conv_ln_g, conv_ln_b,
                   ssm_a_re, ssm_a_im, ssm_b_re, ssm_b_im, ssm_c_re, ssm_c_im, ssm_log_dt, ssm_d, ssm_glu_w,
                   ssm_glu_b, mla_q_norm_g, mla_w_uq, mla_kv_norm_g, mla_w_ukv, w_out, ln1_g, ln1_b,
                   peer_w_q, peer_k1, peer_k2, peer_u, peer_v, ple_w_p, ple_w_g, ln2_g, ln2_b):
    row = lambda v: v[i].reshape(1, -1).astype(F32)
    H = MLA_HEADS
    wuq = mla_w_uq[i].reshape(MLA_Q_RANK, H, MLA_DK)
    wuq = jnp.pad(wuq, ((0, 0), (0, 0), (0, MLA_DKP - MLA_DK))).reshape(MLA_Q_RANK, H * MLA_DKP)
    wukv = mla_w_ukv[i].reshape(MLA_KV_RANK, H, MLA_NOPE + MLA_V)
    wukv = jnp.concatenate([wukv[:, :, :MLA_NOPE].reshape(MLA_KV_RANK, H * MLA_NOPE),
                            wukv[:, :, MLA_NOPE:].reshape(MLA_KV_RANK, H * MLA_V)], axis=1)
    bm, cm, c8, c4 = _ssm_params(ssm_a_re[i], ssm_a_im[i], ssm_b_re[i], ssm_b_im[i], ssm_c_re[i], ssm_c_im[i],
                                 ssm_log_dt[i])
    return {
        "w_in": jnp.pad(w_in[i], ((0, 0), (0, IN_PAD - w_in.shape[2]))).astype(BF16),
        "q_norm_g": row(mla_q_norm_g), "w_uq": wuq.astype(BF16),
        "kv_norm_g": row(mla_kv_norm_g), "w_ukv": wukv.astype(BF16),
        "conv_w": conv_w[i].astype(F32), "conv_b": row(conv_b),
        "conv_ln_g": row(conv_ln_g), "conv_ln_b": row(conv_ln_b),
        "ssm_bm": bm, "ssm_cm": cm, "ssm_coef8": c8, "ssm_coef4": c4,
        "ssm_d": row(ssm_d), "ssm_glu_w": ssm_glu_w[i].astype(BF16), "ssm_glu_b": row(ssm_glu_b),
        "w_out": w_out[i].astype(BF16), "ln1_g": row(ln1_g), "ln1_b": row(ln1_b),
        "peer_w_q": peer_w_q[i].astype(BF16), "peer_k1": peer_k1[i].astype(F32), "peer_k2": peer_k2[i].astype(F32),
        "peer_u": peer_u[i].astype(BF16), "peer_vt": peer_v[i].T.astype(BF16),
        "ple_w_p": ple_w_p[i].astype(BF16), "ple_w_g": ple_w_g[i].astype(BF16),
        "ln2_g": row(ln2_g), "ln2_b": row(ln2_b),
    }


def _run(x, p, layers, ln_emb):
    B, L, D = x.shape
    assert B in (SUBLANE // 2, SUBLANE), "the S5 scan packs 4 or 8 batch rows per sublane group"
    rope_tabs = _rope_tables(L)
    h = x
    for i, lw in enumerate(layers):
        if i == 0:
            h, zc, zs, q, k, v = _in_proj(h, lw, rope_tabs, ln=ln_emb)
        else:
            zc, zs, q, k, v = _in_proj(h, lw, rope_tabs)
        o_mla = _attention(q, k, v)
        o_conv = _conv(zc, lw)
        yf, yb = _ssm(zs.reshape(L * B, SSM_CH), lw, B)
        h1, h1b, h1t = _mix(h, o_conv, zs, yf.reshape(L, B * SSM_CH), yb.reshape(L, B * SSM_CH), o_mla, lw)
        h1 = h1.reshape(B * L, D)
        h1b = h1b.reshape(B * L, D)
        aux = _route(h1b, lw)
        peer_out = _peer(h1t, aux, lw)
        h = _ple(h1, h1b, peer_out, p[i].reshape(B * L, PLE_DIM), lw).reshape(B, L, D)
    return h


def kernel(x_prompt, x_sample, p_prompt, p_sample, ln_emb_g, ln_emb_b, w_in, conv_w, conv_b, conv_ln_g, conv_ln_b, ssm_a_re, ssm_a_im, ssm_b_re, ssm_b_im, ssm_c_re, ssm_c_im, ssm_log_dt, ssm_d, ssm_glu_w, ssm_glu_b, mla_q_norm_g, mla_w_uq, mla_kv_norm_g, mla_w_ukv, w_out, ln1_g, ln1_b, peer_w_q, peer_k1, peer_k2, peer_u, peer_v, ple_w_p, ple_w_g, ln2_g, ln2_b):
    layers = [
        _layer_weights(i, w_in, conv_w, conv_b, conv_ln_g, conv_ln_b,
                       ssm_a_re, ssm_a_im, ssm_b_re, ssm_b_im, ssm_c_re, ssm_c_im, ssm_log_dt, ssm_d, ssm_glu_w,
                       ssm_glu_b, mla_q_norm_g, mla_w_uq, mla_kv_norm_g, mla_w_ukv, w_out, ln1_g, ln1_b,
                       peer_w_q, peer_k1, peer_k2, peer_u, peer_v, ple_w_p, ple_w_g, ln2_g, ln2_b)
        for i in range(DEPTH)
    ]
    ln_emb = (ln_emb_g.reshape(1, -1).astype(F32), ln_emb_b.reshape(1, -1).astype(F32))
    y_prompt = _run(x_prompt, p_prompt, layers, ln_emb)
    y_sample = _run(x_sample, p_sample, layers, ln_emb)
    return (y_prompt, y_sample)
```

```python
import functools
import math

import jax
import jax.numpy as jnp
import numpy as np
from jax import lax
from jax.experimental import pallas as pl
from jax.experimental.pallas import tpu as pltpu

F32 = jnp.float32
BF16 = jnp.bfloat16

D_MODEL = 1024
DEPTH = 2
CONV_CH = 256
CONV_WIDTH = 31
CONV_HALO = 16
SSM_CH = 256
SSM_GROUP = 16
SSM_GROUPS = 16
SSM_STATE = 64
SSM_LANES = SSM_GROUPS * SSM_STATE
MLA_HEADS = 4
MLA_NOPE = 128
MLA_ROPE = 64
MLA_V = 128
MLA_DK = MLA_NOPE + MLA_ROPE
MLA_DKP = 256
MLA_Q_RANK = 512
MLA_KV_RANK = 256
ROPE_THETA = 10000.0
PEER_HEADS = 8
PEER_NKEYS = 128
PEER_EXPERTS = PEER_NKEYS * PEER_NKEYS
PEER_HALF = 128
PEER_TOPK = 16
PLE_DIM = 256
DEEPNORM_ALPHA = float((2 * DEPTH) ** 0.25)
LN_EPS = 1e-5
RMS_EPS = 1e-6
IN_PAD = 1664

LANE = 128
SUBLANE = 8
BF16_ROWS = 16
PACKED_KEYS = 64
VMEM_LIMIT = 56 * 1024 * 1024

_CAND_N = PEER_TOPK + 1
_CAND_PAIRS = tuple((i, j) for i in range(_CAND_N) for j in range(_CAND_N) if (i + 1) * (j + 1) <= _CAND_N)
_CAND_ROWS = -(-len(_CAND_PAIRS) // SUBLANE) * SUBLANE


def _cparams(sem, flags=None):
    return pltpu.CompilerParams(dimension_semantics=sem, vmem_limit_bytes=VMEM_LIMIT, flags=flags)


def _sigmoid(x):
    return 1.0 / (1.0 + jnp.exp(-x))


def _gelu(x):
    return 0.5 * x * (1.0 + jnp.tanh(math.sqrt(2.0 / math.pi) * (x + 0.044715 * (x * x * x))))


def _layer_norm(x, g, b):
    mu = jnp.mean(x, axis=-1, keepdims=True)
    xc = x - mu
    var = jnp.mean(xc * xc, axis=-1, keepdims=True)
    return xc * lax.rsqrt(var + LN_EPS) * g + b


def _rms_norm(x, g):
    return x * lax.rsqrt(jnp.mean(x * x, axis=-1, keepdims=True) + RMS_EPS) * g


def _dot(a, b):
    return jnp.dot(a, b, preferred_element_type=F32)


def _dot_nt(a, b):
    return lax.dot_general(a, b, (((1,), (1,)), ((), ())), preferred_element_type=F32)


def _in_proj_kernel(*refs, pre_ln):
    if pre_ln:
        (x_ref, g_ref, b_ref, w_ref, qg_ref, wuq_ref, kvg_ref, wukv_ref, rc_ref, rs1_ref, rs2_ref,
         h_ref, zc_ref, zs_ref, q_ref, k_ref, v_ref) = refs
    else:
        (x_ref, w_ref, qg_ref, wuq_ref, kvg_ref, wukv_ref, rc_ref, rs1_ref, rs2_ref,
         zc_ref, zs_ref, q_ref, k_ref, v_ref) = refs
    x = x_ref[...]
    if pre_ln:
        x = _layer_norm(x, g_ref[...], b_ref[...])
        h_ref[...] = x
    z = _dot(x.astype(BF16), w_ref[...])
    zc_ref[...] = z[:, 0:512]
    zs_ref[...] = z[:, 512:768]
    cq = _rms_norm(z[:, 768:1280], qg_ref[...])
    ckv = _rms_norm(z[:, 1280:1536], kvg_ref[...])
    kr = z[:, 1536:1664]
    q = _dot(cq.astype(BF16), wuq_ref[...])
    kv = _dot(ckv.astype(BF16), wukv_ref[...])
    rc, rs1, rs2 = rc_ref[...], rs1_ref[...], rs2_ref[...]

    def rope(t):
        return t * rc + pltpu.roll(t, 96, 1) * rs1 + pltpu.roll(t, 32, 1) * rs2

    scale = MLA_DK ** -0.5
    k_rope = rope(kr).astype(BF16)
    for h in range(MLA_HEADS):
        base = h * MLA_DKP
        q_ref[h, :, 0:128] = (q[:, base:base + 128] * scale).astype(BF16)
        q_ref[h, :, 128:256] = (rope(q[:, base + 128:base + 256]) * scale).astype(BF16)
        k_ref[h, :, 0:128] = kv[:, h * 128:(h + 1) * 128].astype(BF16)
        k_ref[h, :, 128:256] = k_rope
        v_ref[h] = kv[:, 512 + h * 128:512 + (h + 1) * 128].astype(BF16)


def _in_proj(x, lw, rope_tabs, ln=None):
    B, L, D = x.shape
    TM = min(512, L)
    nt = L // TM
    pre_ln = ln is not None
    row = lambda b, j: (b, j, 0)
    const2 = lambda b, j: (0, 0)
    in_specs = [pl.BlockSpec((None, TM, D), row)]
    args = [x]
    if pre_ln:
        in_specs += [pl.BlockSpec((1, D), const2)] * 2
        args += [ln[0], ln[1]]
    in_specs += [
        pl.BlockSpec((D, IN_PAD), const2),
        pl.BlockSpec((1, MLA_Q_RANK), const2),
        pl.BlockSpec((MLA_Q_RANK, MLA_HEADS * MLA_DKP), const2),
        pl.BlockSpec((1, MLA_KV_RANK), const2),
        pl.BlockSpec((MLA_KV_RANK, 2 * MLA_HEADS * 128), const2),
    ] + [pl.BlockSpec((TM, LANE), lambda b, j: (j, 0))] * 3
    args += [lw["w_in"], lw["q_norm_g"], lw["w_uq"], lw["kv_norm_g"], lw["w_ukv"], *rope_tabs]
    out_shape, out_specs = [], []
    if pre_ln:
        out_shape.append(jax.ShapeDtypeStruct((B, L, D), F32))
        out_specs.append(pl.BlockSpec((None, TM, D), row))
    out_shape += [
        jax.ShapeDtypeStruct((B, L, 2 * CONV_CH), F32),
        jax.ShapeDtypeStruct((L, B * SSM_CH), F32),
        jax.ShapeDtypeStruct((B, MLA_HEADS, L, MLA_DKP), BF16),
        jax.ShapeDtypeStruct((B, MLA_HEADS, L, MLA_DKP), BF16),
        jax.ShapeDtypeStruct((B, MLA_HEADS, L, MLA_V), BF16),
    ]
    hrow = lambda b, j: (b, 0, j, 0)
    out_specs += [
        pl.BlockSpec((None, TM, 2 * CONV_CH), row),
        pl.BlockSpec((TM, SSM_CH), lambda b, j: (j, b)),
        pl.BlockSpec((None, MLA_HEADS, TM, MLA_DKP), hrow),
        pl.BlockSpec((None, MLA_HEADS, TM, MLA_DKP), hrow),
        pl.BlockSpec((None, MLA_HEADS, TM, MLA_V), hrow),
    ]
    return pl.pallas_call(
        functools.partial(_in_proj_kernel, pre_ln=pre_ln),
        grid=(B, nt), in_specs=in_specs, out_specs=out_specs, out_shape=out_shape,
        compiler_params=_cparams(("parallel", "parallel")), name="in_proj",
    )(*args)


def _attn_kernel(q_ref, k_ref, v_ref, o_ref, m_ref, l_ref, acc_ref, s_ref, p_ref, *, nk, tk):
    j = pl.program_id(3)

    @pl.when(j == 0)
    def _():
        m_ref[...] = jnp.full(m_ref.shape, -jnp.inf, F32)
        l_ref[...] = jnp.zeros(l_ref.shape, F32)
        acc_ref[...] = jnp.zeros(acc_ref.shape, F32)

    s_ref[...] = _dot_nt(q_ref[...], k_ref[...])
    chunks = [pl.ds(c * LANE, LANE) for c in range(tk // LANE)]
    m_part = s_ref[:, chunks[0]]
    for ch in chunks[1:]:
        m_part = jnp.maximum(m_part, s_ref[:, ch])
    m_prev = m_ref[...]
    m_new = jnp.maximum(m_prev, jnp.max(m_part, axis=1, keepdims=True))
    alpha = jnp.exp(m_prev - m_new)
    l_part = jnp.zeros(m_new.shape, F32)
    for ch in chunks:
        p = jnp.exp(s_ref[:, ch] - m_new)
        p_ref[:, ch] = p.astype(BF16)
        l_part = l_part + p
    l_ref[...] = alpha * l_ref[...] + jnp.sum(l_part, axis=1, keepdims=True)
    acc_ref[...] = alpha * acc_ref[...] + _dot(p_ref[...], v_ref[...])
    m_ref[...] = m_new

    @pl.when(j == nk - 1)
    def _():
        o_ref[...] = (acc_ref[...] / l_ref[...]).astype(o_ref.dtype)


def _attention(q, k, v):
    B, H, L, _ = q.shape
    TQ = min(1024, L)
    TK = min(2048, L)
    nq, nk = L // TQ, L // TK
    return pl.pallas_call(
        functools.partial(_attn_kernel, nk=nk, tk=TK),
        grid=(B, H, nq, nk),
        in_specs=[
            pl.BlockSpec((None, None, TQ, MLA_DKP), lambda b, h, i, j: (b, h, i, 0)),
            pl.BlockSpec((None, None, TK, MLA_DKP), lambda b, h, i, j: (b, h, j, 0)),
            pl.BlockSpec((None, None, TK, MLA_V), lambda b, h, i, j: (b, h, j, 0)),
        ],
        out_specs=pl.BlockSpec((None, TQ, MLA_V), lambda b, h, i, j: (b, i, h)),
        out_shape=jax.ShapeDtypeStruct((B, L, H * MLA_V), BF16),
        scratch_shapes=[pltpu.VMEM((TQ, LANE), F32), pltpu.VMEM((TQ, LANE), F32),
                        pltpu.VMEM((TQ, MLA_V), F32), pltpu.VMEM((TQ, TK), F32), pltpu.VMEM((TQ, TK), BF16)],
        compiler_params=_cparams(("parallel", "parallel", "parallel", "arbitrary")), name="attn",
    )(q, k, v)


def _conv_kernel(zm_ref, zp_ref, zn_ref, w_ref, cb_ref, g_ref, b_ref, o_ref, hp_ref, sh_ref, *, tc, nt, rc):
    j = pl.program_id(1)

    def glu(z):
        return z[:, :CONV_CH] * _sigmoid(z[:, CONV_CH:])

    hp_ref[0:CONV_HALO, :] = jnp.where(j > 0, glu(zp_ref[...]), 0.0)
    hp_ref[CONV_HALO:CONV_HALO + tc, :] = glu(zm_ref[...])
    hp_ref[CONV_HALO + tc:2 * CONV_HALO + tc, :] = jnp.where(j < nt - 1, glu(zn_ref[...]), 0.0)
    shift = CONV_HALO - CONV_WIDTH // 2
    span = -(-(CONV_WIDTH + shift) // SUBLANE) * SUBLANE
    for a in range(SUBLANE):
        sh_ref[a] = hp_ref[a:a + tc + span - SUBLANE, :]
    for r in range(tc // rc):
        acc = jnp.zeros((rc, CONV_CH), F32)
        for a in range(SUBLANE):
            for mm in range(span // SUBLANE):
                kk = mm * SUBLANE + a - shift
                if 0 <= kk < CONV_WIDTH:
                    lo = r * rc + mm * SUBLANE
                    acc = acc + sh_ref[a, lo:lo + rc, :] * w_ref[kk:kk + 1, :]
        y = _layer_norm(acc + cb_ref[...], g_ref[...], b_ref[...])
        o_ref[r * rc:(r + 1) * rc, :] = (y * _sigmoid(y)).astype(o_ref.dtype)


def _conv(zc, lw):
    B, L, _ = zc.shape
    TC = min(512, L)
    nt = L // TC
    hb = TC // CONV_HALO
    nhb = L // CONV_HALO
    const2 = lambda b, j: (0, 0)
    return pl.pallas_call(
        functools.partial(_conv_kernel, tc=TC, nt=nt, rc=min(64, TC)),
        grid=(B, nt),
        in_specs=[
            pl.BlockSpec((None, TC, 2 * CONV_CH), lambda b, j: (b, j, 0)),
            pl.BlockSpec((None, CONV_HALO, 2 * CONV_CH), lambda b, j: (b, jnp.maximum(j * hb - 1, 0), 0)),
            pl.BlockSpec((None, CONV_HALO, 2 * CONV_CH), lambda b, j: (b, jnp.minimum((j + 1) * hb, nhb - 1), 0)),
            pl.BlockSpec((CONV_WIDTH, CONV_CH), const2),
            pl.BlockSpec((1, CONV_CH), const2),
            pl.BlockSpec((1, CONV_CH), const2),
            pl.BlockSpec((1, CONV_CH), const2),
        ],
        out_specs=pl.BlockSpec((None, TC, CONV_CH), lambda b, j: (b, j, 0)),
        out_shape=jax.ShapeDtypeStruct((B, L, CONV_CH), BF16),
        scratch_shapes=[pltpu.VMEM((TC + 2 * CONV_HALO, CONV_CH), F32),
                        pltpu.VMEM((SUBLANE, TC + 2 * CONV_HALO - SUBLANE, CONV_CH), F32)],
        compiler_params=_cparams(("parallel", "parallel")), name="conv",
    )(zc, zc, zc, lw["conv_w"], lw["conv_b"], lw["conv_ln_g"], lw["conv_ln_b"])


def _ssm_kernel(uf_ref, ub_ref, bm_ref, coef_ref, cm_ref, yf_ref, yb_ref,
                sf_ref, sb_ref, hf_ref, hb_ref, *, nb, steps, lc):
    i = pl.program_id(0)
    S = SSM_LANES

    @pl.when(i == 0)
    def _():
        hf_ref[...] = jnp.zeros(hf_ref.shape, F32)
        hb_ref[...] = jnp.zeros(hb_ref.shape, F32)

    sf_ref[...] = _dot(uf_ref[...].astype(BF16), bm_ref[0])
    sb_ref[...] = _dot(ub_ref[...].astype(BF16), bm_ref[1])

    lo_half = lax.broadcasted_iota(jnp.int32, (SUBLANE, lc), 0) < nb

    for c0 in range(0, S, lc):
        re_l = pl.ds(c0, lc)
        im_l = pl.ds(S + c0, lc)
        cf = [coef_ref[0, k, :, c0:c0 + lc] for k in range(4)]
        cb = [coef_ref[1, k, :, c0:c0 + lc] for k in range(4)]

        def body(t, carry, re_l=re_l, im_l=im_l, cf=cf, cb=cb):
            hfr, hfi, hbr, hbi = carry
            rf = pl.ds(pl.multiple_of(t * SUBLANE, SUBLANE), SUBLANE)
            rb = pl.ds(pl.multiple_of((steps - 1 - t) * SUBLANE, SUBLANE), SUBLANE)
            out = []
            for ref, rows, c, hr, hi, fwd in ((sf_ref, rf, cf, hfr, hfi, True), (sb_ref, rb, cb, hbr, hbi, False)):
                xr = ref[rows, re_l]
                xi = ref[rows, im_l]
                nr = c[0] * hr - c[1] * hi + xr
                ni = c[0] * hi + c[1] * hr + xi
                if nb != SUBLANE:
                    sxr = pltpu.roll(xr, nb, 0)
                    sxi = pltpu.roll(xi, nb, 0)
                    nr = nr + (c[2] * sxr - c[3] * sxi)
                    ni = ni + (c[2] * sxi + c[3] * sxr)
                ref[rows, re_l] = nr
                ref[rows, im_l] = ni
                if nb != SUBLANE:
                    snr = pltpu.roll(nr, nb, 0)
                    sni = pltpu.roll(ni, nb, 0)
                    if fwd:
                        nr = jnp.where(lo_half, snr, nr)
                        ni = jnp.where(lo_half, sni, ni)
                    else:
                        nr = jnp.where(lo_half, nr, snr)
                        ni = jnp.where(lo_half, ni, sni)
                out += [nr, ni]
            return tuple(out)

        init = (hf_ref[:, re_l], hf_ref[:, im_l], hb_ref[:, re_l], hb_ref[:, im_l])
        hfr, hfi, hbr, hbi = lax.fori_loop(0, steps, body, init, unroll=2)
        hf_ref[:, re_l] = hfr
        hf_ref[:, im_l] = hfi
        hb_ref[:, re_l] = hbr
        hb_ref[:, im_l] = hbi

    yf_ref[...] = _dot(sf_ref[...].astype(BF16), cm_ref[0])
    yb_ref[...] = _dot(sb_ref[...].astype(BF16), cm_ref[1])


def _ssm(u_rows, lw, nb):
    N = u_rows.shape[0]
    R = min(1024, N)
    n = N // R
    steps = R // SUBLANE
    S2 = 2 * SSM_LANES
    return pl.pallas_call(
        functools.partial(_ssm_kernel, nb=nb, steps=steps, lc=512),
        grid=(n,),
        in_specs=[
            pl.BlockSpec((R, SSM_CH), lambda i: (i, 0)),
            pl.BlockSpec((R, SSM_CH), lambda i: (n - 1 - i, 0)),
            pl.BlockSpec((2, SSM_CH, S2), lambda i: (0, 0, 0)),
            pl.BlockSpec((2, 4, SUBLANE, SSM_LANES), lambda i: (0, 0, 0, 0)),
            pl.BlockSpec((2, S2, SSM_CH), lambda i: (0, 0, 0)),
        ],
        out_specs=[
            pl.BlockSpec((R, SSM_CH), lambda i: (i, 0)),
            pl.BlockSpec((R, SSM_CH), lambda i: (n - 1 - i, 0)),
        ],
        out_shape=[jax.ShapeDtypeStruct((N, SSM_CH), F32)] * 2,
        scratch_shapes=[pltpu.VMEM((R, S2), F32), pltpu.VMEM((R, S2), F32),
                        pltpu.VMEM((SUBLANE, S2), F32), pltpu.VMEM((SUBLANE, S2), F32)],
        compiler_params=_cparams(("arbitrary",)), name="ssm",
    )(u_rows, u_rows, lw["ssm_bm"], lw["ssm_coef" + str(nb)], lw["ssm_cm"])


def _mix_kernel(h_ref, oc_ref, u_ref, yf_ref, yb_ref, om_ref, d_ref, gw_ref, gb_ref, wo_ref, g_ref, b_ref,
                h1_ref, h1b_ref, h1t_ref):
    y = d_ref[...] * u_ref[...] + yf_ref[...] + yb_ref[...]
    g = _gelu(y)
    o_ssm = g * _sigmoid(_dot(g.astype(BF16), gw_ref[...]) + gb_ref[...])
    mix = (_dot(oc_ref[...], wo_ref[0:256, :]) + _dot(o_ssm.astype(BF16), wo_ref[256:512, :])
           + _dot(om_ref[...], wo_ref[512:1024, :]))
    h1 = _layer_norm(DEEPNORM_ALPHA * h_ref[...] + mix, g_ref[...], b_ref[...])
    h1_ref[...] = h1
    h1b_ref[...] = h1.astype(BF16)
    h1t_ref[...] = h1.T.astype(BF16)


def _mix(h, o_conv, zs, yf, yb, o_mla, lw):
    B, L, D = h.shape
    TM = min(512, L)
    nt = L // TM
    row = lambda b, j: (b, j, 0)
    tb = lambda b, j: (j, b)
    const2 = lambda b, j: (0, 0)
    return pl.pallas_call(
        _mix_kernel,
        grid=(B, nt),
        in_specs=[
            pl.BlockSpec((None, TM, D), row),
            pl.BlockSpec((None, TM, CONV_CH), row),
            pl.BlockSpec((TM, SSM_CH), tb),
            pl.BlockSpec((TM, SSM_CH), tb),
            pl.BlockSpec((TM, SSM_CH), tb),
            pl.BlockSpec((None, TM, MLA_HEADS * MLA_V), row),
            pl.BlockSpec((1, SSM_CH), const2),
            pl.BlockSpec((SSM_CH, SSM_CH), const2),
            pl.BlockSpec((1, SSM_CH), const2),
            pl.BlockSpec((D, D), const2),
            pl.BlockSpec((1, D), const2),
            pl.BlockSpec((1, D), const2),
        ],
        out_specs=[pl.BlockSpec((None, TM, D), row), pl.BlockSpec((None, TM, D), row),
                   pl.BlockSpec((D, TM), lambda b, j: (0, b * nt + j))],
        out_shape=[jax.ShapeDtypeStruct((B, L, D), F32), jax.ShapeDtypeStruct((B, L, D), BF16),
                   jax.ShapeDtypeStruct((D, B * L), BF16)],
        compiler_params=_cparams(("parallel", "parallel")), name="mix",
    )(h, o_conv, zs, yf, yb, o_mla, lw["ssm_d"], lw["ssm_glu_w"], lw["ssm_glu_b"], lw["w_out"],
      lw["ln1_g"], lw["ln1_b"])


def _top_desc(s, n):
    m = jnp.max(s, axis=0, keepdims=True)
    out = [m]
    for _ in range(n - 1):
        m = jnp.max(jnp.where(s < m, s, -jnp.inf), axis=0, keepdims=True)
        out.append(m)
    return out


def _oddeven_merge_sort_pairs(n):
    pairs, p = [], 1
    while p < n:
        k = p
        while k >= 1:
            for j in range(k % p, n - k, 2 * k):
                for i in range(min(k, n - j - k)):
                    if (i + j) // (2 * p) == (i + j + k) // (2 * p):
                        pairs.append((i + j, i + j + k))
            k //= 2
        p *= 2
    return tuple(pairs)


def _top_desc_keys(s, n):
    groups = s.shape[0] // SUBLANE
    r = [s[SUBLANE * i:SUBLANE * (i + 1), :] for i in range(groups)]
    for a, b in _oddeven_merge_sort_pairs(groups):
        r[a], r[b] = jnp.maximum(r[a], r[b]), jnp.minimum(r[a], r[b])
    ninf = jnp.full(r[0].shape, -jnp.inf, F32)
    out = []
    for k in range(n):
        m = jnp.max(r[0], axis=0, keepdims=True)
        out.append(m)
        if k == n - 1:
            break
        won = r[0] == m
        for i in range(min(groups, n - 1 - k)):
            r[i] = jnp.where(won, r[i + 1] if i + 1 < groups else ninf, r[i])
    return out


def _split_bf16(x):
    hi = x.astype(BF16)
    return hi, (x - hi.astype(F32)).astype(BF16)


def _route_kernel(x_ref, wq_ref, k1_ref, k2_ref, r1_ref, e1_ref, rank2_ref, e2_ref, cand_ref):
    q = _dot(x_ref[...], wq_ref[...])
    k_hl = [_split_bf16(k1_ref[...]), _split_bf16(k2_ref[...])]
    cand_ref[...] = jnp.full(cand_ref.shape, -jnp.inf, F32)
    for h in range(PEER_HEADS):
        s, tops = [], []
        for half in range(2):
            lo = h * 2 * PEER_HALF + half * PEER_HALF
            q_hi, q_lo = _split_bf16(q[:, lo:lo + PEER_HALF])
            k_hi, k_lo = k_hl[half]
            sc = _dot_nt(k_hi, q_hi) + (_dot_nt(k_hi, q_lo) + _dot_nt(k_lo, q_hi))
            s.append(sc)
            tops.append(_top_desc_keys(sc, _CAND_N))
        for r, (a, b) in enumerate(_CAND_PAIRS):
            cand_ref[r:r + 1, :] = tops[0][a] + tops[1][b]
        cs = _top_desc(cand_ref[...], _CAND_N)
        thr = 0.5 * (cs[PEER_TOPK - 1] + cs[PEER_TOPK])
        z = jnp.ones_like(thr)
        for kk in range(1, PEER_TOPK):
            z = z + jnp.exp(cs[kk] - cs[0])
        theta = thr - s[0]
        rank2 = jnp.zeros(s[1].shape, F32)
        r1 = jnp.zeros(s[0].shape, F32)
        for kk in range(PEER_TOPK):
            rank2 = rank2 + jnp.where(s[1] < tops[1][kk], 1.0, 0.0)
            r1 = r1 + jnp.where(tops[1][kk] > theta, 1.0, 0.0)
        outs = ((r1_ref, r1), (e1_ref, jnp.exp(s[0] - tops[0][0]) / z),
                (rank2_ref, rank2), (e2_ref, jnp.exp(s[1] - tops[1][0])))
        for ref, val in outs:
            for c in range(val.shape[1] // LANE):
                tile = val[:, c * LANE:(c + 1) * LANE]
                if len(ref.shape) == 2:
                    words = pltpu.bitcast(tile.astype(BF16), jnp.uint32)
                    ref[pl.ds((c * PEER_HEADS + h) * PACKED_KEYS, PACKED_KEYS), :] = words
                else:
                    ref[h, c] = tile


def _route(xb, lw):
    T, D = xb.shape
    TT = min(256, T)
    aux_f32 = jax.ShapeDtypeStruct((PEER_HEADS, T // LANE, PEER_NKEYS, LANE), F32)
    aux_bf16 = jax.ShapeDtypeStruct((T // LANE * PEER_HEADS * PACKED_KEYS, LANE), jnp.uint32)
    f32_spec = pl.BlockSpec((PEER_HEADS, TT // LANE, PEER_NKEYS, LANE), lambda i: (0, i, 0, 0))
    bf16_spec = pl.BlockSpec((TT // LANE * PEER_HEADS * PACKED_KEYS, LANE), lambda i: (i, 0))
    return pl.pallas_call(
        _route_kernel,
        grid=(T // TT,),
        in_specs=[
            pl.BlockSpec((TT, D), lambda i: (i, 0)),
            pl.BlockSpec((D, 2 * PEER_HEADS * PEER_HALF), lambda i: (0, 0)),
            pl.BlockSpec((PEER_NKEYS, PEER_HALF), lambda i: (0, 0)),
            pl.BlockSpec((PEER_NKEYS, PEER_HALF), lambda i: (0, 0)),
        ],
        out_specs=[f32_spec, f32_spec, bf16_spec, bf16_spec],
        out_shape=[aux_f32, aux_f32, aux_bf16, aux_bf16],
        scratch_shapes=[pltpu.VMEM((_CAND_ROWS, TT), F32)],
        compiler_params=_cparams(("parallel",)), name="route",
    )(xb, lw["peer_w_q"], lw["peer_k1"], lw["peer_k2"])


MXU_TILE = 256


def _peer_kernel(xt_ref, u_ref, vt_ref, r1_ref, e1_ref, rank2_ref, e2_ref, o_ref, act_ref, ag_ref, acc_ref,
                 *, ne, eb, tt):
    e = pl.program_id(1)

    @pl.when(e == 0)
    def _():
        acc_ref[...] = jnp.zeros(acc_ref.shape, F32)

    n1 = eb // PEER_NKEYS
    rows = pl.ds(pl.multiple_of(e * n1, n1), n1)
    per_tile = MXU_TILE // LANE
    for tn in range(tt // MXU_TILE):
        tcols = pl.ds(tn * MXU_TILE, MXU_TILE)
        for tk in range(eb // MXU_TILE):
            trows = pl.ds(tk * MXU_TILE, MXU_TILE)
            act_ref[trows, tcols] = _dot(u_ref[trows, :], xt_ref[:, tcols])
            for c in range(tn * per_tile, (tn + 1) * per_tile):
                cols = pl.ds(c * LANE, LANE)
                rr = [r1_ref[h, c, rows, :] for h in range(PEER_HEADS)]
                ee = [e1_ref[h, c, rows, :] for h in range(PEER_HEADS)]
                for i in range(tk * per_tile, (tk + 1) * per_tile):
                    gate = jnp.zeros((PEER_NKEYS // BF16_ROWS, BF16_ROWS, LANE), BF16)
                    for h in range(PEER_HEADS):
                        r16 = jnp.broadcast_to(rr[h][i:i + 1, :], (BF16_ROWS, LANE)).astype(BF16)
                        e16 = jnp.broadcast_to(ee[h][i:i + 1, :], (BF16_ROWS, LANE)).astype(BF16)
                        tile = pl.ds((c * PEER_HEADS + h) * PACKED_KEYS, PACKED_KEYS)
                        rk = pltpu.bitcast(rank2_ref[tile, :], BF16).reshape(gate.shape)
                        e2 = pltpu.bitcast(e2_ref[tile, :], BF16).reshape(gate.shape)
                        gate = gate + jnp.where(rk < r16[None], e2 * e16[None], jnp.zeros((), BF16))
                    for g in range(PEER_NKEYS // BF16_ROWS):
                        er = pl.ds(i * PEER_NKEYS + g * BF16_ROWS, BF16_ROWS)
                        ag_ref[er, cols] = _gelu(act_ref[er, cols].astype(BF16)) * gate[g]
    acc_ref[...] += _dot(vt_ref[...], ag_ref[...])

    @pl.when(e == ne - 1)
    def _():
        o_ref[...] = acc_ref[...].T


def _peer(xt, aux, lw):
    D, T = xt.shape
    TT = min(512, T)
    EB = 2048
    ne = PEER_EXPERTS // EB
    f32_spec = pl.BlockSpec((PEER_HEADS, TT // LANE, PEER_NKEYS, LANE), lambda i, e: (0, i, 0, 0))
    bf16_spec = pl.BlockSpec((TT // LANE * PEER_HEADS * PACKED_KEYS, LANE), lambda i, e: (i, 0))
    return pl.pallas_call(
        functools.partial(_peer_kernel, ne=ne, eb=EB, tt=TT),
        grid=(T // TT, ne),
        in_specs=[
            pl.BlockSpec((D, TT), lambda i, e: (0, i)),
            pl.BlockSpec((EB, D), lambda i, e: (e, 0)),
            pl.BlockSpec((D, EB), lambda i, e: (0, e)),
            f32_spec, f32_spec, bf16_spec, bf16_spec,
        ],
        out_specs=pl.BlockSpec((TT, D), lambda i, e: (i, 0)),
        out_shape=jax.ShapeDtypeStruct((T, D), F32),
        scratch_shapes=[pltpu.VMEM((EB, TT), F32), pltpu.VMEM((EB, TT), BF16), pltpu.VMEM((D, TT), F32)],
        compiler_params=_cparams(("parallel", "arbitrary")), name="peer",
    )(xt, lw["peer_u"], lw["peer_vt"], *aux)


def _ple_kernel(h1_ref, h1b_ref, pe_ref, p_ref, wg_ref, wp_ref, g_ref, b_ref, o_ref):
    gate = _sigmoid(_dot(h1b_ref[...], wg_ref[...]))
    emb = _dot(p_ref[...].astype(BF16), wp_ref[...])
    r = DEEPNORM_ALPHA * h1_ref[...] + pe_ref[...] + gate * emb
    o_ref[...] = _layer_norm(r, g_ref[...], b_ref[...])


def _ple(h1, h1b, peer_out, p, lw):
    T, D = h1.shape
    TM = min(512, T)
    row = lambda i: (i, 0)
    const = lambda i: (0, 0)
    return pl.pallas_call(
        _ple_kernel,
        grid=(T // TM,),
        in_specs=[
            pl.BlockSpec((TM, D), row), pl.BlockSpec((TM, D), row), pl.BlockSpec((TM, D), row),
            pl.BlockSpec((TM, PLE_DIM), row),
            pl.BlockSpec((D, D), const), pl.BlockSpec((PLE_DIM, D), const),
            pl.BlockSpec((1, D), const), pl.BlockSpec((1, D), const),
        ],
        out_specs=pl.BlockSpec((TM, D), row),
        out_shape=jax.ShapeDtypeStruct((T, D), F32),
        compiler_params=_cparams(("parallel",)), name="ple",
    )(h1, h1b, peer_out, p, lw["ple_w_g"], lw["ple_w_p"], lw["ln2_g"], lw["ln2_b"])


def _ssm_params(a_re, a_im, b_re, b_im, c_re, c_im, log_dt):
    G, P, C = SSM_GROUPS, SSM_STATE, SSM_GROUP
    ar, ai = a_re.astype(F32), a_im.astype(F32)
    dt = jnp.exp(log_dt.astype(F32))[..., None]
    mag = jnp.exp(dt * ar)
    abr, abi = mag * jnp.cos(dt * ai), mag * jnp.sin(dt * ai)
    den = ar * ar + ai * ai
    qr = ((abr - 1.0) * ar + abi * ai) / den
    qi = (abi * ar - (abr - 1.0) * ai) / den
    br, bi = b_re.astype(F32), b_im.astype(F32)
    bbr = qr[..., None] * br - qi[..., None] * bi
    bbi = qr[..., None] * bi + qi[..., None] * br
    eye = jnp.eye(G, dtype=F32)

    def bdiag_in(m):
        return jnp.einsum("dgpc,gh->dgchp", m, eye).reshape(2, G * C, G * P)

    def bdiag_out(m):
        return jnp.einsum("dgcp,gh->dgphc", m, eye).reshape(2, G * P, G * C)

    bm = jnp.concatenate([bdiag_in(bbr), bdiag_in(bbi)], axis=2).astype(BF16)
    cm = jnp.concatenate([bdiag_out(c_re.astype(F32)), -bdiag_out(c_im.astype(F32))], axis=1).astype(BF16)
    a1 = (abr.reshape(2, G * P), abi.reshape(2, G * P))
    a2 = (a1[0] * a1[0] - a1[1] * a1[1], 2.0 * a1[0] * a1[1])
    zero = (jnp.zeros_like(a1[0]), jnp.zeros_like(a1[0]))
    pick = lambda z, d: (z[0][d:d + 1], z[1][d:d + 1])

    def rows(top, bottom):
        shape = (top[0].shape[0], SUBLANE // 2, G * P)
        return [jnp.concatenate([jnp.broadcast_to(t[:, None, :], shape),
                                 jnp.broadcast_to(b[:, None, :], shape)], axis=1) for t, b in zip(top, bottom)]

    c8 = jnp.stack([*rows(a1, a1), *rows(zero, zero)], axis=1)
    f1, f2 = rows(pick(a1, 0), pick(a2, 0)), rows(pick(zero, 0), pick(a1, 0))
    b1, b2 = rows(pick(a2, 1), pick(a1, 1)), rows(pick(a1, 1), pick(zero, 1))
    c4 = jnp.concatenate([jnp.stack([*f1, *f2], axis=1), jnp.stack([*b1, *b2], axis=1)], axis=0)
    return bm, cm, c8.astype(F32), c4.astype(F32)


def _rope_tables(L):
    pos = jnp.arange(L, dtype=F32)
    inv_freq = ROPE_THETA ** (-jnp.arange(0, MLA_ROPE, 2, dtype=F32) / MLA_ROPE)
    ang = pos[:, None] * inv_freq[None, :]
    cos, sin = jnp.cos(ang), jnp.sin(ang)
    z = jnp.zeros_like(cos)
    return (jnp.concatenate([cos, cos, z, z], axis=1),
            jnp.concatenate([-sin, z, z, z], axis=1),
            jnp.concatenate([z, sin, z, z], axis=1))


def _layer_weights(i, w_in, conv_w, conv_b, conv_ln_g, conv_ln_b,
                   ssm_a_re, ssm_a_im, ssm_b_re, ssm_b_im, ssm_c_re, ssm_c_im, ssm_log_dt, ssm_d, ssm_glu_w,
                   ssm_glu_b, mla_q_norm_g, mla_w_uq, mla_kv_norm_g, mla_w_ukv, w_out, ln1_g, ln1_b,
                   peer_w_q, peer_k1, peer_k2, peer_u, peer_v, ple_w_p, ple_w_g, ln2_g, ln2_b):
    row = lambda v: v[i].reshape(1, -1).astype(F32)
    H = MLA_HEADS
    wuq = mla_w_uq[i].reshape(MLA_Q_RANK, H, MLA_DK)
    wuq = jnp.pad(wuq, ((0, 0), (0, 0), (0, MLA_DKP - MLA_DK))).reshape(MLA_Q_RANK, H * MLA_DKP)
    wukv = mla_w_ukv[i].reshape(MLA_KV_RANK, H, MLA_NOPE + MLA_V)
    wukv = jnp.concatenate([wukv[:, :, :MLA_NOPE].reshape(MLA_KV_RANK, H * MLA_NOPE),
                            wukv[:, :, MLA_NOPE:].reshape(MLA_KV_RANK, H * MLA_V)], axis=1)
    bm, cm, c8, c4 = _ssm_params(ssm_a_re[i], ssm_a_im[i], ssm_b_re[i], ssm_b_im[i], ssm_c_re[i], ssm_c_im[i],
                                 ssm_log_dt[i])
    return {
        "w_in": jnp.pad(w_in[i], ((0, 0), (0, IN_PAD - w_in.shape[2]))).astype(BF16),
        "q_norm_g": row(mla_q_norm_g), "w_uq": wuq.astype(BF16),
        "kv_norm_g": row(mla_kv_norm_g), "w_ukv": wukv.astype(BF16),
        "conv_w": conv_w[i].astype(F32), "conv_b": row(conv_b),
        "conv_ln_g": row(conv_ln_g), "conv_ln_b": row(conv_ln_b),
        "ssm_bm": bm, "ssm_cm": cm, "ssm_coef8": c8, "ssm_coef4": c4,
        "ssm_d": row(ssm_d), "ssm_glu_w": ssm_glu_w[i].astype(BF16), "ssm_glu_b": row(ssm_glu_b),
        "w_out": w_out[i].astype(BF16), "ln1_g": row(ln1_g), "ln1_b": row(ln1_b),
        "peer_w_q": peer_w_q[i].astype(BF16), "peer_k1": peer_k1[i].astype(F32), "peer_k2": peer_k2[i].astype(F32),
        "peer_u": peer_u[i].astype(BF16), "peer_vt": peer_v[i].T.astype(BF16),
        "ple_w_p": ple_w_p[i].astype(BF16), "ple_w_g": ple_w_g[i].astype(BF16),
        "ln2_g": row(ln2_g), "ln2_b": row(ln2_b),
    }


def _run(x, p, layers, ln_emb):
    B, L, D = x.shape
    assert B in (SUBLANE // 2, SUBLANE), "the S5 scan packs 4 or 8 batch rows per sublane group"
    rope_tabs = _rope_tables(L)
    h = x
    for i, lw in enumerate(layers):
        if i == 0:
            h, zc, zs, q, k, v = _in_proj(h, lw, rope_tabs, ln=ln_emb)
        else:
            zc, zs, q, k, v = _in_proj(h, lw, rope_tabs)
        o_mla = _attention(q, k, v)
        o_conv = _conv(zc, lw)
        yf, yb = _ssm(zs.reshape(L * B, SSM_CH), lw, B)
        h1, h1b, h1t = _mix(h, o_conv, zs, yf.reshape(L, B * SSM_CH), yb.reshape(L, B * SSM_CH), o_mla, lw)
        h1 = h1.reshape(B * L, D)
        h1b = h1b.reshape(B * L, D)
        aux = _route(h1b, lw)
        peer_out = _peer(h1t, aux, lw)
        h = _ple(h1, h1b, peer_out, p[i].reshape(B * L, PLE_DIM), lw).reshape(B, L, D)
    return h


def kernel(x_prompt, x_sample, p_prompt, p_sample, ln_emb_g, ln_emb_b, w_in, conv_w, conv_b, conv_ln_g, conv_ln_b, ssm_a_re, ssm_a_im, ssm_b_re, ssm_b_im, ssm_c_re, ssm_c_im, ssm_log_dt, ssm_d, ssm_glu_w, ssm_glu_b, mla_q_norm_g, mla_w_uq, mla_kv_norm_g, mla_w_ukv, w_out, ln1_g, ln1_b, peer_w_q, peer_k1, peer_k2, peer_u, peer_v, ple_w_p, ple_w_g, ln2_g, ln2_b):
    layers = [
        _layer_weights(i, w_in, conv_w, conv_b, conv_ln_g, conv_ln_b,
                       ssm_a_re, ssm_a_im, ssm_b_re, ssm_b_im, ssm_c_re, ssm_c_im, ssm_log_dt, ssm_d, ssm_glu_w,
                       ssm_glu_b, mla_q_norm_g, mla_w_uq, mla_kv_norm_g, mla_w_ukv, w_out, ln1_g, ln1_b,
                       peer_w_q, peer_k1, peer_k2, peer_u, peer_v, ple_w_p, ple_w_g, ln2_g, ln2_b)
        for i in range(DEPTH)
    ]
    ln_emb = (ln_emb_g.reshape(1, -1).astype(F32), ln_emb_b.reshape(1, -1).astype(F32))
    y_prompt = _run(x_prompt, p_prompt, layers, ln_emb)
    y_sample = _run(x_sample, p_sample, layers, ln_emb)
    return (y_prompt, y_sample)
```

```python
import functools
import math

import jax
import jax.numpy as jnp
from jax import lax
from jax.experimental import pallas as pl
from jax.experimental.pallas import tpu as pltpu

F32 = jnp.float32
BF16 = jnp.bfloat16

D_MODEL = 1024
DEPTH = 2
CONV_CH = 256
CONV_WIDTH = 31
CONV_HALO = 16
SSM_CH = 256
SSM_GROUP = 16
SSM_GROUPS = 16
SSM_STATE = 64
SSM_LANES = SSM_GROUPS * SSM_STATE
MLA_HEADS = 4
MLA_NOPE = 128
MLA_ROPE = 64
MLA_V = 128
MLA_DK = MLA_NOPE + MLA_ROPE
MLA_DKP = 256
MLA_Q_RANK = 512
MLA_KV_RANK = 256
ROPE_THETA = 10000.0
PEER_HEADS = 8
PEER_NKEYS = 128
PEER_EXPERTS = PEER_NKEYS * PEER_NKEYS
PEER_HALF = 128
PEER_TOPK = 16
PLE_DIM = 256
DEEPNORM_ALPHA = float((2 * DEPTH) ** 0.25)
LN_EPS = 1e-5
RMS_EPS = 1e-6
LANE = 128
SUBLANE = 8
BF16_ROWS = 2 * SUBLANE
PACKED_KEYS = PEER_NKEYS // 2
VMEM_LIMIT = 56 * 1024 * 1024
IN_PAD = 2 * CONV_CH + SSM_CH + MLA_Q_RANK + MLA_KV_RANK + LANE

_CAND_N = PEER_TOPK + 1
_CAND_PAIRS = tuple((i, j) for i in range(_CAND_N) for j in range(_CAND_N) if (i + 1) * (j + 1) <= _CAND_N)
_CAND_ROWS = -(-len(_CAND_PAIRS) // SUBLANE) * SUBLANE


def _cparams(sem):
    return pltpu.CompilerParams(dimension_semantics=sem, vmem_limit_bytes=VMEM_LIMIT)


def _sigmoid(x):
    return 1.0 / (1.0 + jnp.exp(-x))


def _gelu(x):
    return 0.5 * x * (1.0 + jnp.tanh(math.sqrt(2.0 / math.pi) * (x + 0.044715 * (x * x * x))))


def _layer_norm(x, g, b):
    mu = jnp.mean(x, axis=-1, keepdims=True)
    xc = x - mu
    var = jnp.mean(xc * xc, axis=-1, keepdims=True)
    return xc * lax.rsqrt(var + LN_EPS) * g + b


def _rms_norm(x, g):
    return x * lax.rsqrt(jnp.mean(x * x, axis=-1, keepdims=True) + RMS_EPS) * g


def _dot(a, b):
    return jnp.dot(a, b, preferred_element_type=F32)


def _dot_nt(a, b):
    return lax.dot_general(a, b, (((1,), (1,)), ((), ())), preferred_element_type=F32)


def _in_proj_kernel(*refs, pre_ln):
    if pre_ln:
        (x_ref, g_ref, b_ref, w_ref, qg_ref, wuq_ref, kvg_ref, wukv_ref, rc_ref, rs1_ref, rs2_ref,
         h_ref, zc_ref, zs_ref, q_ref, k_ref, v_ref) = refs
    else:
        (x_ref, w_ref, qg_ref, wuq_ref, kvg_ref, wukv_ref, rc_ref, rs1_ref, rs2_ref,
         zc_ref, zs_ref, q_ref, k_ref, v_ref) = refs
    x = x_ref[...]
    if pre_ln:
        x = _layer_norm(x, g_ref[...], b_ref[...])
        h_ref[...] = x
    z = _dot(x.astype(BF16), w_ref[...])
    zc_ref[...] = z[:, 0:512]
    zs_ref[...] = z[:, 512:768]
    cq = _rms_norm(z[:, 768:1280], qg_ref[...])
    ckv = _rms_norm(z[:, 1280:1536], kvg_ref[...])
    kr = z[:, 1536:1664]
    q = _dot(cq.astype(BF16), wuq_ref[...])
    kv = _dot(ckv.astype(BF16), wukv_ref[...])
    rc, rs1, rs2 = rc_ref[...], rs1_ref[...], rs2_ref[...]

    def rope(t):
        return t * rc + pltpu.roll(t, 96, 1) * rs1 + pltpu.roll(t, 32, 1) * rs2

    scale = MLA_DK ** -0.5 * math.log2(math.e)
    k_rope = rope(kr).astype(BF16)
    for h in range(MLA_HEADS):
        base = h * MLA_DKP
        q_ref[h, :, 0:128] = (q[:, base:base + 128] * scale).astype(BF16)
        q_ref[h, :, 128:256] = (rope(q[:, base + 128:base + 256]) * scale).astype(BF16)
        k_ref[h, :, 0:128] = kv[:, h * 128:(h + 1) * 128].astype(BF16)
        k_ref[h, :, 128:256] = k_rope
        v_ref[h] = kv[:, 512 + h * 128:512 + (h + 1) * 128].astype(BF16)


def _in_proj(x, lw, rope_tabs, ln=None):
    B, L, D = x.shape
    TM = min(1024, L)
    nt = L // TM
    pre_ln = ln is not None
    row = lambda b, j: (b, j, 0)
    const2 = lambda b, j: (0, 0)
    in_specs = [pl.BlockSpec((None, TM, D), row)]
    args = [x]
    if pre_ln:
        in_specs += [pl.BlockSpec((1, D), const2)] * 2
        args += [ln[0], ln[1]]
    in_specs += [
        pl.BlockSpec((D, IN_PAD), const2),
        pl.BlockSpec((1, MLA_Q_RANK), const2),
        pl.BlockSpec((MLA_Q_RANK, MLA_HEADS * MLA_DKP), const2),
        pl.BlockSpec((1, MLA_KV_RANK), const2),
        pl.BlockSpec((MLA_KV_RANK, 2 * MLA_HEADS * 128), const2),
    ] + [pl.BlockSpec((TM, LANE), lambda b, j: (j, 0))] * 3
    args += [lw["w_in"], lw["q_norm_g"], lw["w_uq"], lw["kv_norm_g"], lw["w_ukv"], *rope_tabs]
    out_shape, out_specs = [], []
    if pre_ln:
        out_shape.append(jax.ShapeDtypeStruct((B, L, D), F32))
        out_specs.append(pl.BlockSpec((None, TM, D), row))
    out_shape += [
        jax.ShapeDtypeStruct((B, L, 2 * CONV_CH), F32),
        jax.ShapeDtypeStruct((L, B * SSM_CH), F32),
        jax.ShapeDtypeStruct((B, MLA_HEADS, L, MLA_DKP), BF16),
        jax.ShapeDtypeStruct((B, MLA_HEADS, L, MLA_DKP), BF16),
        jax.ShapeDtypeStruct((B, MLA_HEADS, L, MLA_V), BF16),
    ]
    hrow = lambda b, j: (b, 0, j, 0)
    out_specs += [
        pl.BlockSpec((None, TM, 2 * CONV_CH), row),
        pl.BlockSpec((TM, SSM_CH), lambda b, j: (j, b)),
        pl.BlockSpec((None, MLA_HEADS, TM, MLA_DKP), hrow),
        pl.BlockSpec((None, MLA_HEADS, TM, MLA_DKP), hrow),
        pl.BlockSpec((None, MLA_HEADS, TM, MLA_V), hrow),
    ]
    return pl.pallas_call(
        functools.partial(_in_proj_kernel, pre_ln=pre_ln),
        grid=(B, nt), in_specs=in_specs, out_specs=out_specs, out_shape=out_shape,
        compiler_params=_cparams(("parallel", "parallel")), name="in_proj",
    )(*args)


def _attn_kernel(q_ref, k_ref, v_ref, o_ref, m_ref, l_ref, acc_ref, s_ref, p_ref, *, nk, tk):
    j = pl.program_id(3)

    @pl.when(j == 0)
    def _():
        m_ref[...] = jnp.full(m_ref.shape, -jnp.inf, F32)
        l_ref[...] = jnp.zeros(l_ref.shape, F32)
        acc_ref[...] = jnp.zeros(acc_ref.shape, F32)

    s_ref[...] = _dot_nt(q_ref[...], k_ref[...])
    chunks = [pl.ds(c * LANE, LANE) for c in range(tk // LANE)]
    m_part = s_ref[:, chunks[0]]
    for ch in chunks[1:]:
        m_part = jnp.maximum(m_part, s_ref[:, ch])
    m_prev = m_ref[...]
    m_new = jnp.maximum(m_prev, jnp.max(m_part, axis=1, keepdims=True))
    alpha = jnp.exp2(m_prev - m_new)
    l_part = jnp.zeros(m_new.shape, F32)
    for ch in chunks:
        p = jnp.exp2(s_ref[:, ch] - m_new)
        p_ref[:, ch] = p.astype(BF16)
        l_part = l_part + p
    l_ref[...] = alpha * l_ref[...] + jnp.sum(l_part, axis=1, keepdims=True)
    acc_ref[...] = alpha * acc_ref[...] + _dot(p_ref[...], v_ref[...])
    m_ref[...] = m_new

    @pl.when(j == nk - 1)
    def _():
        o_ref[...] = (acc_ref[...] / l_ref[...]).astype(o_ref.dtype)


def _attention(q, k, v):
    B, H, L, _ = q.shape
    TQ = min(2048, L)
    TK = min(2048, L)
    nq, nk = L // TQ, L // TK
    return pl.pallas_call(
        functools.partial(_attn_kernel, nk=nk, tk=TK),
        grid=(B, H, nq, nk),
        in_specs=[
            pl.BlockSpec((None, None, TQ, MLA_DKP), lambda b, h, i, j: (b, h, i, 0)),
            pl.BlockSpec((None, None, TK, MLA_DKP), lambda b, h, i, j: (b, h, j, 0)),
            pl.BlockSpec((None, None, TK, MLA_V), lambda b, h, i, j: (b, h, j, 0)),
        ],
        out_specs=pl.BlockSpec((None, TQ, MLA_V), lambda b, h, i, j: (b, i, h)),
        out_shape=jax.ShapeDtypeStruct((B, L, H * MLA_V), BF16),
        scratch_shapes=[pltpu.VMEM((TQ, LANE), F32), pltpu.VMEM((TQ, LANE), F32),
                        pltpu.VMEM((TQ, MLA_V), F32), pltpu.VMEM((TQ, TK), F32), pltpu.VMEM((TQ, TK), BF16)],
        compiler_params=_cparams(("parallel", "parallel", "parallel", "arbitrary")), name="attn",
    )(q, k, v)


def _conv_kernel(zm_ref, zp_ref, zn_ref, w_ref, cb_ref, g_ref, b_ref, o_ref, hp_ref, sh_ref, *, tc, nt, rc):
    j = pl.program_id(1)

    def glu(z):
        return z[:, :CONV_CH] * _sigmoid(z[:, CONV_CH:])

    hp_ref[0:CONV_HALO, :] = jnp.where(j > 0, glu(zp_ref[...]), 0.0)
    hp_ref[CONV_HALO:CONV_HALO + tc, :] = glu(zm_ref[...])
    hp_ref[CONV_HALO + tc:2 * CONV_HALO + tc, :] = jnp.where(j < nt - 1, glu(zn_ref[...]), 0.0)
    shift = CONV_HALO - CONV_WIDTH // 2
    span = -(-(CONV_WIDTH + shift) // SUBLANE) * SUBLANE
    for a in range(SUBLANE):
        sh_ref[a] = hp_ref[a:a + tc + span - SUBLANE, :]
    for r in range(tc // rc):
        acc = jnp.zeros((rc, CONV_CH), F32)
        for a in range(SUBLANE):
            for mm in range(span // SUBLANE):
                kk = mm * SUBLANE + a - shift
                if 0 <= kk < CONV_WIDTH:
                    lo = r * rc + mm * SUBLANE
                    acc = acc + sh_ref[a, lo:lo + rc, :] * w_ref[kk:kk + 1, :]
        y = _layer_norm(acc + cb_ref[...], g_ref[...], b_ref[...])
        o_ref[r * rc:(r + 1) * rc, :] = (y * _sigmoid(y)).astype(o_ref.dtype)


def _conv(zc, lw):
    B, L, _ = zc.shape
    TC = min(512, L)
    nt = L // TC
    hb = TC // CONV_HALO
    nhb = L // CONV_HALO
    const2 = lambda b, j: (0, 0)
    return pl.pallas_call(
        functools.partial(_conv_kernel, tc=TC, nt=nt, rc=min(64, TC)),
        grid=(B, nt),
        in_specs=[
            pl.BlockSpec((None, TC, 2 * CONV_CH), lambda b, j: (b, j, 0)),
            pl.BlockSpec((None, CONV_HALO, 2 * CONV_CH), lambda b, j: (b, jnp.maximum(j * hb - 1, 0), 0)),
            pl.BlockSpec((None, CONV_HALO, 2 * CONV_CH), lambda b, j: (b, jnp.minimum((j + 1) * hb, nhb - 1), 0)),
            pl.BlockSpec((CONV_WIDTH, CONV_CH), const2),
            pl.BlockSpec((1, CONV_CH), const2),
            pl.BlockSpec((1, CONV_CH), const2),
            pl.BlockSpec((1, CONV_CH), const2),
        ],
        out_specs=pl.BlockSpec((None, TC, CONV_CH), lambda b, j: (b, j, 0)),
        out_shape=jax.ShapeDtypeStruct((B, L, CONV_CH), BF16),
        scratch_shapes=[pltpu.VMEM((TC + 2 * CONV_HALO, CONV_CH), F32),
                        pltpu.VMEM((SUBLANE, TC + 2 * CONV_HALO - SUBLANE, CONV_CH), F32)],
        compiler_params=_cparams(("parallel", "parallel")), name="conv",
    )(zc, zc, zc, lw["conv_w"], lw["conv_b"], lw["conv_ln_g"], lw["conv_ln_b"])


def _ssm_kernel(uf_ref, ub_ref, bm_ref, coef_ref, cm_ref, yf_ref, yb_ref,
                sf_ref, sb_ref, hf_ref, hb_ref, *, nb, steps, lc):
    i = pl.program_id(0)
    S = SSM_LANES

    @pl.when(i == 0)
    def _():
        hf_ref[...] = jnp.zeros(hf_ref.shape, F32)
        hb_ref[...] = jnp.zeros(hb_ref.shape, F32)

    sf_ref[...] = _dot(uf_ref[...].astype(BF16), bm_ref[0])
    sb_ref[...] = _dot(ub_ref[...].astype(BF16), bm_ref[1])

    lo_half = lax.broadcasted_iota(jnp.int32, (SUBLANE, lc), 0) < nb

    for c0 in range(0, S, lc):
        re_l = pl.ds(c0, lc)
        im_l = pl.ds(S + c0, lc)
        cf = [coef_ref[0, k, :, c0:c0 + lc] for k in range(4)]
        cb = [coef_ref[1, k, :, c0:c0 + lc] for k in range(4)]

        def body(t, carry, re_l=re_l, im_l=im_l, cf=cf, cb=cb):
            hfr, hfi, hbr, hbi = carry
            rf = pl.ds(pl.multiple_of(t * SUBLANE, SUBLANE), SUBLANE)
            rb = pl.ds(pl.multiple_of((steps - 1 - t) * SUBLANE, SUBLANE), SUBLANE)
            out = []
            for ref, rows, c, hr, hi, fwd in ((sf_ref, rf, cf, hfr, hfi, True), (sb_ref, rb, cb, hbr, hbi, False)):
                xr = ref[rows, re_l]
                xi = ref[rows, im_l]
                nr = c[0] * hr - c[1] * hi + xr
                ni = c[0] * hi + c[1] * hr + xi
                if nb != SUBLANE:
                    sxr = pltpu.roll(xr, nb, 0)
                    sxi = pltpu.roll(xi, nb, 0)
                    nr = nr + (c[2] * sxr - c[3] * sxi)
                    ni = ni + (c[2] * sxi + c[3] * sxr)
                ref[rows, re_l] = nr
                ref[rows, im_l] = ni
                if nb != SUBLANE:
                    snr = pltpu.roll(nr, nb, 0)
                    sni = pltpu.roll(ni, nb, 0)
                    if fwd:
                        nr = jnp.where(lo_half, snr, nr)
                        ni = jnp.where(lo_half, sni, ni)
                    else:
                        nr = jnp.where(lo_half, nr, snr)
                        ni = jnp.where(lo_half, ni, sni)
                out += [nr, ni]
            return tuple(out)

        init = (hf_ref[:, re_l], hf_ref[:, im_l], hb_ref[:, re_l], hb_ref[:, im_l])
        hfr, hfi, hbr, hbi = lax.fori_loop(0, steps, body, init, unroll=2)
        hf_ref[:, re_l] = hfr
        hf_ref[:, im_l] = hfi
        hb_ref[:, re_l] = hbr
        hb_ref[:, im_l] = hbi

    yf_ref[...] = _dot(sf_ref[...].astype(BF16), cm_ref[0])
    yb_ref[...] = _dot(sb_ref[...].astype(BF16), cm_ref[1])


def _ssm(u_rows, lw, nb):
    N = u_rows.shape[0]
    R = min(1024, N)
    n = N // R
    steps = R // SUBLANE
    S2 = 2 * SSM_LANES
    return pl.pallas_call(
        functools.partial(_ssm_kernel, nb=nb, steps=steps, lc=512),
        grid=(n,),
        in_specs=[
            pl.BlockSpec((R, SSM_CH), lambda i: (i, 0)),
            pl.BlockSpec((R, SSM_CH), lambda i: (n - 1 - i, 0)),
            pl.BlockSpec((2, SSM_CH, S2), lambda i: (0, 0, 0)),
            pl.BlockSpec((2, 4, SUBLANE, SSM_LANES), lambda i: (0, 0, 0, 0)),
            pl.BlockSpec((2, S2, SSM_CH), lambda i: (0, 0, 0)),
        ],
        out_specs=[
            pl.BlockSpec((R, SSM_CH), lambda i: (i, 0)),
            pl.BlockSpec((R, SSM_CH), lambda i: (n - 1 - i, 0)),
        ],
        out_shape=[jax.ShapeDtypeStruct((N, SSM_CH), F32)] * 2,
        scratch_shapes=[pltpu.VMEM((R, S2), F32), pltpu.VMEM((R, S2), F32),
                        pltpu.VMEM((SUBLANE, S2), F32), pltpu.VMEM((SUBLANE, S2), F32)],
        compiler_params=_cparams(("arbitrary",)), name="ssm",
    )(u_rows, u_rows, lw["ssm_bm"], lw["ssm_coef" + str(nb)], lw["ssm_cm"])


def _mix_kernel(h_ref, oc_ref, u_ref, yf_ref, yb_ref, om_ref, d_ref, gw_ref, gb_ref, wo_ref, g_ref, b_ref,
                h1_ref, h1b_ref, h1t_ref):
    y = d_ref[...] * u_ref[...] + yf_ref[...] + yb_ref[...]
    g = _gelu(y)
    o_ssm = g * _sigmoid(_dot(g.astype(BF16), gw_ref[...]) + gb_ref[...])
    mix = (_dot(oc_ref[...], wo_ref[0:256, :]) + _dot(o_ssm.astype(BF16), wo_ref[256:512, :])
           + _dot(om_ref[...], wo_ref[512:1024, :]))
    h1 = _layer_norm(DEEPNORM_ALPHA * h_ref[...] + mix, g_ref[...], b_ref[...])
    h1_ref[...] = h1
    h1b_ref[...] = h1.astype(BF16)
    h1t_ref[...] = h1.T.astype(BF16)


def _mix(h, o_conv, zs, yf, yb, o_mla, lw):
    B, L, D = h.shape
    TM = min(1024, L)
    nt = L // TM
    row = lambda b, j: (b, j, 0)
    tb = lambda b, j: (j, b)
    const2 = lambda b, j: (0, 0)
    return pl.pallas_call(
        _mix_kernel,
        grid=(B, nt),
        in_specs=[
            pl.BlockSpec((None, TM, D), row),
            pl.BlockSpec((None, TM, CONV_CH), row),
            pl.BlockSpec((TM, SSM_CH), tb),
            pl.BlockSpec((TM, SSM_CH), tb),
            pl.BlockSpec((TM, SSM_CH), tb),
            pl.BlockSpec((None, TM, MLA_HEADS * MLA_V), row),
            pl.BlockSpec((1, SSM_CH), const2),
            pl.BlockSpec((SSM_CH, SSM_CH), const2),
            pl.BlockSpec((1, SSM_CH), const2),
            pl.BlockSpec((D, D), const2),
            pl.BlockSpec((1, D), const2),
            pl.BlockSpec((1, D), const2),
        ],
        out_specs=[pl.BlockSpec((None, TM, D), row), pl.BlockSpec((None, TM, D), row),
                   pl.BlockSpec((D, TM), lambda b, j: (0, b * nt + j))],
        out_shape=[jax.ShapeDtypeStruct((B, L, D), F32), jax.ShapeDtypeStruct((B, L, D), BF16),
                   jax.ShapeDtypeStruct((D, B * L), BF16)],
        compiler_params=_cparams(("parallel", "parallel")), name="mix",
    )(h, o_conv, zs, yf, yb, o_mla, lw["ssm_d"], lw["ssm_glu_w"], lw["ssm_glu_b"], lw["w_out"],
      lw["ln1_g"], lw["ln1_b"])


def _top_desc(s, n):
    m = jnp.max(s, axis=0, keepdims=True)
    out = [m]
    for _ in range(n - 1):
        m = jnp.max(jnp.where(s < m, s, -jnp.inf), axis=0, keepdims=True)
        out.append(m)
    return out


def _oddeven_merge_sort_pairs(n):
    pairs, p = [], 1
    while p < n:
        k = p
        while k >= 1:
            for j in range(k % p, n - k, 2 * k):
                for i in range(min(k, n - j - k)):
                    if (i + j) // (2 * p) == (i + j + k) // (2 * p):
                        pairs.append((i + j, i + j + k))
            k //= 2
        p *= 2
    return tuple(pairs)


def _top_desc_keys(s, n):
    groups = s.shape[0] // SUBLANE
    r = [s[SUBLANE * i:SUBLANE * (i + 1), :] for i in range(groups)]
    for a, b in _oddeven_merge_sort_pairs(groups):
        r[a], r[b] = jnp.maximum(r[a], r[b]), jnp.minimum(r[a], r[b])
    ninf = jnp.full(r[0].shape, -jnp.inf, F32)
    out = []
    for k in range(n):
        m = jnp.max(r[0], axis=0, keepdims=True)
        out.append(m)
        if k == n - 1:
            break
        won = r[0] == m
        for i in range(min(groups, n - 1 - k)):
            r[i] = jnp.where(won, r[i + 1] if i + 1 < groups else ninf, r[i])
    return out


def _split_bf16(x):
    hi = x.astype(BF16)
    return hi, (x - hi.astype(F32)).astype(BF16)


def _route_kernel(x_ref, wq_ref, k1_ref, k2_ref, r1_ref, e1_ref, rank2_ref, e2_ref, cand_ref):
    q = _dot(x_ref[...], wq_ref[...])
    k_hl = [_split_bf16(k1_ref[...]), _split_bf16(k2_ref[...])]
    cand_ref[...] = jnp.full(cand_ref.shape, -jnp.inf, F32)
    for h in range(PEER_HEADS):
        s, tops = [], []
        for half in range(2):
            lo = h * 2 * PEER_HALF + half * PEER_HALF
            q_hi, q_lo = _split_bf16(q[:, lo:lo + PEER_HALF])
            k_hi, k_lo = k_hl[half]
            sc = _dot_nt(k_hi, q_hi) + (_dot_nt(k_hi, q_lo) + _dot_nt(k_lo, q_hi))
            s.append(sc)
            tops.append(_top_desc_keys(sc, _CAND_N))
        for r, (a, b) in enumerate(_CAND_PAIRS):
            cand_ref[r:r + 1, :] = tops[0][a] + tops[1][b]
        cs = _top_desc(cand_ref[...], _CAND_N)
        thr = 0.5 * (cs[PEER_TOPK - 1] + cs[PEER_TOPK])
        z = jnp.ones_like(thr)
        for kk in range(1, PEER_TOPK):
            z = z + jnp.exp(cs[kk] - cs[0])
        theta = thr - s[0]
        rank2 = jnp.zeros(s[1].shape, F32)
        r1 = jnp.zeros(s[0].shape, F32)
        for kk in range(PEER_TOPK):
            rank2 = rank2 + jnp.where(s[1] < tops[1][kk], 1.0, 0.0)
            r1 = r1 + jnp.where(tops[1][kk] > theta, 1.0, 0.0)
        e1 = jnp.exp(s[0] - tops[0][0]) / z
        e2 = jnp.exp(s[1] - tops[1][0])
        for c in range(s[0].shape[1] // LANE):
            cols = slice(c * LANE, (c + 1) * LANE)
            for ref, val in ((r1_ref, r1), (e1_ref, e1)):
                bits = pltpu.bitcast(val[:, cols].astype(BF16).astype(F32), jnp.uint32)
                ref[h, c] = bits | (bits >> 16)
            for ref, val in ((rank2_ref, rank2), (e2_ref, e2)):
                words = pltpu.bitcast(val[:, cols].astype(BF16), jnp.uint32)
                ref[pl.ds((c * PEER_HEADS + h) * PACKED_KEYS, PACKED_KEYS), :] = words


def _route(xb, lw):
    T, D = xb.shape
    TT = min(256, T)
    aux_f32 = jax.ShapeDtypeStruct((PEER_HEADS, T // LANE, PEER_NKEYS, LANE), jnp.uint32)
    aux_bf16 = jax.ShapeDtypeStruct((T // LANE * PEER_HEADS * PACKED_KEYS, LANE), jnp.uint32)
    f32_spec = pl.BlockSpec((PEER_HEADS, TT // LANE, PEER_NKEYS, LANE), lambda i: (0, i, 0, 0))
    bf16_spec = pl.BlockSpec((TT // LANE * PEER_HEADS * PACKED_KEYS, LANE), lambda i: (i, 0))
    return pl.pallas_call(
        _route_kernel,
        grid=(T // TT,),
        in_specs=[
            pl.BlockSpec((TT, D), lambda i: (i, 0)),
            pl.BlockSpec((D, 2 * PEER_HEADS * PEER_HALF), lambda i: (0, 0)),
            pl.BlockSpec((PEER_NKEYS, PEER_HALF), lambda i: (0, 0)),
            pl.BlockSpec((PEER_NKEYS, PEER_HALF), lambda i: (0, 0)),
        ],
        out_specs=[f32_spec, f32_spec, bf16_spec, bf16_spec],
        out_shape=[aux_f32, aux_f32, aux_bf16, aux_bf16],
        scratch_shapes=[pltpu.VMEM((_CAND_ROWS, TT), F32)],
        compiler_params=_cparams(("parallel",)), name="route",
    )(xb, lw["peer_w_q"], lw["peer_k1"], lw["peer_k2"])


MXU_TILE = 256


def _peer_kernel(xt_ref, u_ref, vt_ref, r1_ref, e1_ref, rank2_ref, e2_ref, o_ref, act_ref, ag_ref, acc_ref,
                 *, ne, eb, tt):
    e = pl.program_id(1)

    @pl.when(e == 0)
    def _():
        acc_ref[...] = jnp.zeros(acc_ref.shape, F32)

    n1 = eb // PEER_NKEYS
    rows = pl.ds(pl.multiple_of(e * n1, n1), n1)
    per_tile = MXU_TILE // LANE
    for tn in range(tt // MXU_TILE):
        tcols = pl.ds(tn * MXU_TILE, MXU_TILE)
        for tk in range(eb // MXU_TILE):
            trows = pl.ds(tk * MXU_TILE, MXU_TILE)
            act_ref[trows, tcols] = _dot(u_ref[trows, :], xt_ref[:, tcols])
            for c in range(tn * per_tile, (tn + 1) * per_tile):
                cols = pl.ds(c * LANE, LANE)
                rr = [r1_ref[h, c, rows, :] for h in range(PEER_HEADS)]
                ee = [e1_ref[h, c, rows, :] for h in range(PEER_HEADS)]
                for i in range(tk * per_tile, (tk + 1) * per_tile):
                    gate = jnp.zeros((PEER_NKEYS // BF16_ROWS, BF16_ROWS, LANE), BF16)
                    for h in range(PEER_HEADS):
                        r16 = pltpu.bitcast(jnp.broadcast_to(rr[h][i:i + 1, :], (SUBLANE, LANE)), BF16)
                        e16 = pltpu.bitcast(jnp.broadcast_to(ee[h][i:i + 1, :], (SUBLANE, LANE)), BF16)
                        tile = pl.ds((c * PEER_HEADS + h) * PACKED_KEYS, PACKED_KEYS)
                        rk = pltpu.bitcast(rank2_ref[tile, :], BF16).reshape(gate.shape)
                        e2 = pltpu.bitcast(e2_ref[tile, :], BF16).reshape(gate.shape)
                        gate = gate + jnp.where(rk < r16[None], e2 * e16[None], jnp.zeros((), BF16))
                    for g in range(PEER_NKEYS // BF16_ROWS):
                        er = pl.ds(i * PEER_NKEYS + g * BF16_ROWS, BF16_ROWS)
                        ag_ref[er, cols] = _gelu(act_ref[er, cols].astype(BF16)) * gate[g]
    acc_ref[...] += _dot(vt_ref[...], ag_ref[...])

    @pl.when(e == ne - 1)
    def _():
        o_ref[...] = acc_ref[...].T


def _peer(xt, aux, lw):
    D, T = xt.shape
    TT = min(512, T)
    EB = 2048
    ne = PEER_EXPERTS // EB
    f32_spec = pl.BlockSpec((PEER_HEADS, TT // LANE, PEER_NKEYS, LANE), lambda i, e: (0, i, 0, 0))
    bf16_spec = pl.BlockSpec((TT // LANE * PEER_HEADS * PACKED_KEYS, LANE), lambda i, e: (i, 0))
    return pl.pallas_call(
        functools.partial(_peer_kernel, ne=ne, eb=EB, tt=TT),
        grid=(T // TT, ne),
        in_specs=[
            pl.BlockSpec((D, TT), lambda i, e: (0, i)),
            pl.BlockSpec((EB, D), lambda i, e: (e, 0)),
            pl.BlockSpec((D, EB), lambda i, e: (0, e)),
            f32_spec, f32_spec, bf16_spec, bf16_spec,
        ],
        out_specs=pl.BlockSpec((TT, D), lambda i, e: (i, 0)),
        out_shape=jax.ShapeDtypeStruct((T, D), F32),
        scratch_shapes=[pltpu.VMEM((EB, TT), F32), pltpu.VMEM((EB, TT), BF16), pltpu.VMEM((D, TT), F32)],
        compiler_params=_cparams(("parallel", "arbitrary")), name="peer",
    )(xt, lw["peer_u"], lw["peer_vt"], *aux)


def _ple_kernel(h1_ref, h1b_ref, pe_ref, p_ref, wg_ref, wp_ref, g_ref, b_ref, o_ref):
    gate = _sigmoid(_dot(h1b_ref[...], wg_ref[...]))
    emb = _dot(p_ref[...].astype(BF16), wp_ref[...])
    r = DEEPNORM_ALPHA * h1_ref[...] + pe_ref[...] + gate * emb
    o_ref[...] = _layer_norm(r, g_ref[...], b_ref[...])


def _ple(h1, h1b, peer_out, p, lw):
    T, D = h1.shape
    TM = min(1024, T)
    row = lambda i: (i, 0)
    const = lambda i: (0, 0)
    return pl.pallas_call(
        _ple_kernel,
        grid=(T // TM,),
        in_specs=[
            pl.BlockSpec((TM, D), row), pl.BlockSpec((TM, D), row), pl.BlockSpec((TM, D), row),
            pl.BlockSpec((TM, PLE_DIM), row),
            pl.BlockSpec((D, D), const), pl.BlockSpec((PLE_DIM, D), const),
            pl.BlockSpec((1, D), const), pl.BlockSpec((1, D), const),
        ],
        out_specs=pl.BlockSpec((TM, D), row),
        out_shape=jax.ShapeDtypeStruct((T, D), F32),
        compiler_params=_cparams(("parallel",)), name="ple",
    )(h1, h1b, peer_out, p, lw["ple_w_g"], lw["ple_w_p"], lw["ln2_g"], lw["ln2_b"])


def _ssm_params(a_re, a_im, b_re, b_im, c_re, c_im, log_dt):
    G, P, C = SSM_GROUPS, SSM_STATE, SSM_GROUP
    ar, ai = a_re.astype(F32), a_im.astype(F32)
    dt = jnp.exp(log_dt.astype(F32))[..., None]
    mag = jnp.exp(dt * ar)
    abr, abi = mag * jnp.cos(dt * ai), mag * jnp.sin(dt * ai)
    den = ar * ar + ai * ai
    qr = ((abr - 1.0) * ar + abi * ai) / den
    qi = (abi * ar - (abr - 1.0) * ai) / den
    br, bi = b_re.astype(F32), b_im.astype(F32)
    bbr = qr[..., None] * br - qi[..., None] * bi
    bbi = qr[..., None] * bi + qi[..., None] * br
    eye = jnp.eye(G, dtype=F32)

    def bdiag_in(m):
        return jnp.einsum("dgpc,gh->dgchp", m, eye).reshape(2, G * C, G * P)

    def bdiag_out(m):
        return jnp.einsum("dgcp,gh->dgphc", m, eye).reshape(2, G * P, G * C)

    bm = jnp.concatenate([bdiag_in(bbr), bdiag_in(bbi)], axis=2).astype(BF16)
    cm = jnp.concatenate([bdiag_out(c_re.astype(F32)), -bdiag_out(c_im.astype(F32))], axis=1).astype(BF16)
    a1 = (abr.reshape(2, G * P), abi.reshape(2, G * P))
    a2 = (a1[0] * a1[0] - a1[1] * a1[1], 2.0 * a1[0] * a1[1])
    zero = (jnp.zeros_like(a1[0]), jnp.zeros_like(a1[0]))
    pick = lambda z, d: (z[0][d:d + 1], z[1][d:d + 1])

    def rows(top, bottom):
        shape = (top[0].shape[0], SUBLANE // 2, G * P)
        return [jnp.concatenate([jnp.broadcast_to(t[:, None, :], shape),
                                 jnp.broadcast_to(b[:, None, :], shape)], axis=1) for t, b in zip(top, bottom)]

    c8 = jnp.stack([*rows(a1, a1), *rows(zero, zero)], axis=1)
    f1, f2 = rows(pick(a1, 0), pick(a2, 0)), rows(pick(zero, 0), pick(a1, 0))
    b1, b2 = rows(pick(a2, 1), pick(a1, 1)), rows(pick(a1, 1), pick(zero, 1))
    c4 = jnp.concatenate([jnp.stack([*f1, *f2], axis=1), jnp.stack([*b1, *b2], axis=1)], axis=0)
    return bm, cm, c8.astype(F32), c4.astype(F32)


def _rope_tables(L):
    pos = jnp.arange(L, dtype=F32)
    inv_freq = ROPE_THETA ** (-jnp.arange(0, MLA_ROPE, 2, dtype=F32) / MLA_ROPE)
    ang = pos[:, None] * inv_freq[None, :]
    cos, sin = jnp.cos(ang), jnp.sin(ang)
    z = jnp.zeros_like(cos)
    return (jnp.concatenate([cos, cos, z, z], axis=1),
            jnp.concatenate([-sin, z, z, z], axis=1),
            jnp.concatenate([z, sin, z, z], axis=1))


def _layer_weights(i, w_in, conv_w, conv_b, conv_ln_g, conv_ln_b,
                   ssm_a_re, ssm_a_im, ssm_b_re, ssm_b_im, ssm_c_re, ssm_c_im, ssm_log_dt, ssm_d, ssm_glu_w,
                   ssm_glu_b, mla_q_norm_g, mla_w_uq, mla_kv_norm_g, mla_w_ukv, w_out, ln1_g, ln1_b,
                   peer_w_q, peer_k1, peer_k2, peer_u, peer_v, ple_w_p, ple_w_g, ln2_g, ln2_b):
    row = lambda v: v[i].reshape(1, -1).astype(F32)
    H = MLA_HEADS
    wuq = mla_w_uq[i].reshape(MLA_Q_RANK, H, MLA_DK)
    wuq = jnp.pad(wuq, ((0, 0), (0, 0), (0, MLA_DKP - MLA_DK))).reshape(MLA_Q_RANK, H * MLA_DKP)
    wukv = mla_w_ukv[i].reshape(MLA_KV_RANK, H, MLA_NOPE + MLA_V)
    wukv = jnp.concatenate([wukv[:, :, :MLA_NOPE].reshape(MLA_KV_RANK, H * MLA_NOPE),
                            wukv[:, :, MLA_NOPE:].reshape(MLA_KV_RANK, H * MLA_V)], axis=1)
    bm, cm, c8, c4 = _ssm_params(ssm_a_re[i], ssm_a_im[i], ssm_b_re[i], ssm_b_im[i], ssm_c_re[i], ssm_c_im[i],
                                 ssm_log_dt[i])
    return {
        "w_in": jnp.pad(w_in[i], ((0, 0), (0, IN_PAD - w_in.shape[2]))).astype(BF16),
        "q_norm_g": row(mla_q_norm_g), "w_uq": wuq.astype(BF16),
        "kv_norm_g": row(mla_kv_norm_g), "w_ukv": wukv.astype(BF16),
        "conv_w": conv_w[i].astype(F32), "conv_b": row(conv_b),
        "conv_ln_g": row(conv_ln_g), "conv_ln_b": row(conv_ln_b),
        "ssm_bm": bm, "ssm_cm": cm, "ssm_coef8": c8, "ssm_coef4": c4,
        "ssm_d": row(ssm_d), "ssm_glu_w": ssm_glu_w[i].astype(BF16), "ssm_glu_b": row(ssm_glu_b),
        "w_out": w_out[i].astype(BF16), "ln1_g": row(ln1_g), "ln1_b": row(ln1_b),
        "peer_w_q": peer_w_q[i].astype(BF16), "peer_k1": peer_k1[i].astype(F32), "peer_k2": peer_k2[i].astype(F32),
        "peer_u": peer_u[i].astype(BF16), "peer_vt": peer_v[i].T.astype(BF16),
        "ple_w_p": ple_w_p[i].astype(BF16), "ple_w_g": ple_w_g[i].astype(BF16),
        "ln2_g": row(ln2_g), "ln2_b": row(ln2_b),
    }


def _run(x, p, layers, ln_emb):
    B, L, D = x.shape
    assert B in (SUBLANE // 2, SUBLANE), "the S5 scan packs 4 or 8 batch rows per sublane group"
    rope_tabs = _rope_tables(L)
    h = x
    for i, lw in enumerate(layers):
        if i == 0:
            h, zc, zs, q, k, v = _in_proj(h, lw, rope_tabs, ln=ln_emb)
        else:
            zc, zs, q, k, v = _in_proj(h, lw, rope_tabs)
        o_mla = _attention(q, k, v)
        o_conv = _conv(zc, lw)
        yf, yb = _ssm(zs.reshape(L * B, SSM_CH), lw, B)
        h1, h1b, h1t = _mix(h, o_conv, zs, yf.reshape(L, B * SSM_CH), yb.reshape(L, B * SSM_CH), o_mla, lw)
        h1 = h1.reshape(B * L, D)
        h1b = h1b.reshape(B * L, D)
        aux = _route(h1b, lw)
        peer_out = _peer(h1t, aux, lw)
        h = _ple(h1, h1b, peer_out, p[i].reshape(B * L, PLE_DIM), lw).reshape(B, L, D)
    return h


def kernel(x_prompt, x_sample, p_prompt, p_sample, ln_emb_g, ln_emb_b, w_in, conv_w, conv_b, conv_ln_g, conv_ln_b, ssm_a_re, ssm_a_im, ssm_b_re, ssm_b_im, ssm_c_re, ssm_c_im, ssm_log_dt, ssm_d, ssm_glu_w, ssm_glu_b, mla_q_norm_g, mla_w_uq, mla_kv_norm_g, mla_w_ukv, w_out, ln1_g, ln1_b, peer_w_q, peer_k1, peer_k2, peer_u, peer_v, ple_w_p, ple_w_g, ln2_g, ln2_b):
    layers = [
        _layer_weights(i, w_in, conv_w, conv_b, conv_ln_g, conv_ln_b,
                       ssm_a_re, ssm_a_im, ssm_b_re, ssm_b_im, ssm_c_re, ssm_c_im, ssm_log_dt, ssm_d, ssm_glu_w,
                       ssm_glu_b, mla_q_norm_g, mla_w_uq, mla_kv_norm_g, mla_w_ukv, w_out, ln1_g, ln1_b,
                       peer_w_q, peer_k1, peer_k2, peer_u, peer_v, ple_w_p, ple_w_g, ln2_g, ln2_b)
        for i in range(DEPTH)
    ]
    ln_emb = (ln_emb_g.reshape(1, -1).astype(F32), ln_emb_b.reshape(1, -1).astype(F32))
    y_prompt = _run(x_prompt, p_prompt, layers, ln_emb)
    y_sample = _run(x_sample, p_sample, layers, ln_emb)
    return (y_prompt, y_sample)
```

```python
import functools
import math

import jax
import jax.numpy as jnp
from jax import lax
from jax.experimental import pallas as pl
from jax.experimental.pallas import tpu as pltpu

F32 = jnp.float32
BF16 = jnp.bfloat16

D_MODEL = 1024
DEPTH = 2
CONV_CH = 256
CONV_WIDTH = 31
CONV_HALO = 16
SSM_CH = 256
SSM_GROUP = 16
SSM_GROUPS = 16
SSM_STATE = 64
SSM_LANES = SSM_GROUPS * SSM_STATE
MLA_HEADS = 4
MLA_NOPE = 128
MLA_ROPE = 64
MLA_V = 128
MLA_DK = MLA_NOPE + MLA_ROPE
MLA_DKP = 256
MLA_Q_RANK = 512
MLA_KV_RANK = 256
ROPE_THETA = 10000.0
PEER_HEADS = 8
PEER_NKEYS = 128
PEER_EXPERTS = PEER_NKEYS * PEER_NKEYS
PEER_HALF = 128
PEER_TOPK = 16
PLE_DIM = 256
DEEPNORM_ALPHA = float((2 * DEPTH) ** 0.25)
LN_EPS = 1e-5
RMS_EPS = 1e-6
LANE = 128
SUBLANE = 8
BF16_ROWS = 2 * SUBLANE
PACKED_KEYS = PEER_NKEYS // 2
VMEM_LIMIT = 56 * 1024 * 1024
IN_PAD = 2 * CONV_CH + SSM_CH + MLA_Q_RANK + MLA_KV_RANK + LANE

_CAND_N = PEER_TOPK + 1
_CAND_PAIRS = tuple((i, j) for i in range(_CAND_N) for j in range(_CAND_N) if (i + 1) * (j + 1) <= _CAND_N)
_CAND_ROWS = -(-len(_CAND_PAIRS) // SUBLANE) * SUBLANE


def _cparams(sem):
    return pltpu.CompilerParams(dimension_semantics=sem, vmem_limit_bytes=VMEM_LIMIT)


def _sigmoid(x):
    return 1.0 / (1.0 + jnp.exp(-x))


def _gelu(x):
    return 0.5 * x * (1.0 + jnp.tanh(math.sqrt(2.0 / math.pi) * (x + 0.044715 * (x * x * x))))


def _layer_norm(x, g, b):
    mu = jnp.mean(x, axis=-1, keepdims=True)
    xc = x - mu
    var = jnp.mean(xc * xc, axis=-1, keepdims=True)
    return xc * lax.rsqrt(var + LN_EPS) * g + b


def _rms_norm(x, g):
    return x * lax.rsqrt(jnp.mean(x * x, axis=-1, keepdims=True) + RMS_EPS) * g


def _dot(a, b):
    return jnp.dot(a, b, preferred_element_type=F32)


def _dot_nt(a, b):
    return lax.dot_general(a, b, (((1,), (1,)), ((), ())), preferred_element_type=F32)


def _in_proj_kernel(*refs, pre_ln):
    if pre_ln:
        (x_ref, g_ref, b_ref, w_ref, qg_ref, wuq_ref, kvg_ref, wukv_ref, rc_ref, rs1_ref, rs2_ref,
         h_ref, zc_ref, zs_ref, q_ref, k_ref, v_ref) = refs
    else:
        (x_ref, w_ref, qg_ref, wuq_ref, kvg_ref, wukv_ref, rc_ref, rs1_ref, rs2_ref,
         zc_ref, zs_ref, q_ref, k_ref, v_ref) = refs
    x = x_ref[...]
    if pre_ln:
        x = _layer_norm(x, g_ref[...], b_ref[...])
        h_ref[...] = x
    z = _dot(x.astype(BF16), w_ref[...])
    zc_ref[...] = z[:, 0:512]
    zs_ref[...] = z[:, 512:768]
    cq = _rms_norm(z[:, 768:1280], qg_ref[...])
    ckv = _rms_norm(z[:, 1280:1536], kvg_ref[...])
    kr = z[:, 1536:1664]
    q = _dot(cq.astype(BF16), wuq_ref[...])
    kv = _dot(ckv.astype(BF16), wukv_ref[...])
    rc, rs1, rs2 = rc_ref[...], rs1_ref[...], rs2_ref[...]

    def rope(t):
        return t * rc + pltpu.roll(t, 96, 1) * rs1 + pltpu.roll(t, 32, 1) * rs2

    scale = MLA_DK ** -0.5 * math.log2(math.e)
    k_rope = rope(kr).astype(BF16)
    for h in range(MLA_HEADS):
        base = h * MLA_DKP
        q_ref[h, :, 0:128] = (q[:, base:base + 128] * scale).astype(BF16)
        q_ref[h, :, 128:256] = (rope(q[:, base + 128:base + 256]) * scale).astype(BF16)
        k_ref[h, :, 0:128] = kv[:, h * 128:(h + 1) * 128].astype(BF16)
        k_ref[h, :, 128:256] = k_rope
        v_ref[h] = kv[:, 512 + h * 128:512 + (h + 1) * 128].astype(BF16)


def _in_proj(x, lw, rope_tabs, ln=None):
    B, L, D = x.shape
    TM = min(1024, L)
    nt = L // TM
    pre_ln = ln is not None
    row = lambda b, j: (b, j, 0)
    const2 = lambda b, j: (0, 0)
    in_specs = [pl.BlockSpec((None, TM, D), row)]
    args = [x]
    if pre_ln:
        in_specs += [pl.BlockSpec((1, D), const2)] * 2
        args += [ln[0], ln[1]]
    in_specs += [
        pl.BlockSpec((D, IN_PAD), const2),
        pl.BlockSpec((1, MLA_Q_RANK), const2),
        pl.BlockSpec((MLA_Q_RANK, MLA_HEADS * MLA_DKP), const2),
        pl.BlockSpec((1, MLA_KV_RANK), const2),
        pl.BlockSpec((MLA_KV_RANK, 2 * MLA_HEADS * 128), const2),
    ] + [pl.BlockSpec((TM, LANE), lambda b, j: (j, 0))] * 3
    args += [lw["w_in"], lw["q_norm_g"], lw["w_uq"], lw["kv_norm_g"], lw["w_ukv"], *rope_tabs]
    out_shape, out_specs = [], []
    if pre_ln:
        out_shape.append(jax.ShapeDtypeStruct((B, L, D), F32))
        out_specs.append(pl.BlockSpec((None, TM, D), row))
    out_shape += [
        jax.ShapeDtypeStruct((B, L, 2 * CONV_CH), F32),
        jax.ShapeDtypeStruct((L, B * SSM_CH), F32),
        jax.ShapeDtypeStruct((B, MLA_HEADS, L, MLA_DKP), BF16),
        jax.ShapeDtypeStruct((B, MLA_HEADS, L, MLA_DKP), BF16),
        jax.ShapeDtypeStruct((B, MLA_HEADS, L, MLA_V), BF16),
    ]
    hrow = lambda b, j: (b, 0, j, 0)
    out_specs += [
        pl.BlockSpec((None, TM, 2 * CONV_CH), row),
        pl.BlockSpec((TM, SSM_CH), lambda b, j: (j, b)),
        pl.BlockSpec((None, MLA_HEADS, TM, MLA_DKP), hrow),
        pl.BlockSpec((None, MLA_HEADS, TM, MLA_DKP), hrow),
        pl.BlockSpec((None, MLA_HEADS, TM, MLA_V), hrow),
    ]
    return pl.pallas_call(
        functools.partial(_in_proj_kernel, pre_ln=pre_ln),
        grid=(B, nt), in_specs=in_specs, out_specs=out_specs, out_shape=out_shape,
        compiler_params=_cparams(("parallel", "parallel")), name="in_proj",
    )(*args)


def _attn_kernel(q_ref, k_ref, v_ref, o_ref, m_ref, l_ref, acc_ref, s_ref, p_ref, *, nk, tk):
    j = pl.program_id(3)

    @pl.when(j == 0)
    def _():
        m_ref[...] = jnp.full(m_ref.shape, -jnp.inf, F32)
        l_ref[...] = jnp.zeros(l_ref.shape, F32)
        acc_ref[...] = jnp.zeros(acc_ref.shape, F32)

    s_ref[...] = _dot_nt(q_ref[...], k_ref[...])
    chunks = [pl.ds(c * LANE, LANE) for c in range(tk // LANE)]
    m_part = s_ref[:, chunks[0]]
    for ch in chunks[1:]:
        m_part = jnp.maximum(m_part, s_ref[:, ch])
    m_prev = m_ref[...]
    m_new = jnp.maximum(m_prev, jnp.max(m_part, axis=1, keepdims=True))
    alpha = jnp.exp2(m_prev - m_new)
    l_part = jnp.zeros(m_new.shape, F32)
    for ch in chunks:
        p = jnp.exp2(s_ref[:, ch] - m_new)
        p_ref[:, ch] = p.astype(BF16)
        l_part = l_part + p
    l_ref[...] = alpha * l_ref[...] + jnp.sum(l_part, axis=1, keepdims=True)
    acc_ref[...] = alpha * acc_ref[...] + _dot(p_ref[...], v_ref[...])
    m_ref[...] = m_new

    @pl.when(j == nk - 1)
    def _():
        o_ref[...] = (acc_ref[...] / l_ref[...]).astype(o_ref.dtype)


def _attention(q, k, v):
    B, H, L, _ = q.shape
    TQ = min(2048, L)
    TK = min(2048, L)
    nq, nk = L // TQ, L // TK
    return pl.pallas_call(
        functools.partial(_attn_kernel, nk=nk, tk=TK),
        grid=(B, H, nq, nk),
        in_specs=[
            pl.BlockSpec((None, None, TQ, MLA_DKP), lambda b, h, i, j: (b, h, i, 0)),
            pl.BlockSpec((None, None, TK, MLA_DKP), lambda b, h, i, j: (b, h, j, 0)),
            pl.BlockSpec((None, None, TK, MLA_V), lambda b, h, i, j: (b, h, j, 0)),
        ],
        out_specs=pl.BlockSpec((None, TQ, MLA_V), lambda b, h, i, j: (b, i, h)),
        out_shape=jax.ShapeDtypeStruct((B, L, H * MLA_V), BF16),
        scratch_shapes=[pltpu.VMEM((TQ, LANE), F32), pltpu.VMEM((TQ, LANE), F32),
                        pltpu.VMEM((TQ, MLA_V), F32), pltpu.VMEM((TQ, TK), F32), pltpu.VMEM((TQ, TK), BF16)],
        compiler_params=_cparams(("parallel", "parallel", "parallel", "arbitrary")), name="attn",
    )(q, k, v)


def _conv_kernel(zm_ref, zp_ref, zn_ref, w_ref, cb_ref, g_ref, b_ref, o_ref, hp_ref, sh_ref, *, tc, nt, rc):
    j = pl.program_id(1)

    def glu(z):
        return z[:, :CONV_CH] * _sigmoid(z[:, CONV_CH:])

    hp_ref[0:CONV_HALO, :] = jnp.where(j > 0, glu(zp_ref[...]), 0.0)
    hp_ref[CONV_HALO:CONV_HALO + tc, :] = glu(zm_ref[...])
    hp_ref[CONV_HALO + tc:2 * CONV_HALO + tc, :] = jnp.where(j < nt - 1, glu(zn_ref[...]), 0.0)
    shift = CONV_HALO - CONV_WIDTH // 2
    span = -(-(CONV_WIDTH + shift) // SUBLANE) * SUBLANE
    for a in range(SUBLANE):
        sh_ref[a] = hp_ref[a:a + tc + span - SUBLANE, :]
    for r in range(tc // rc):
        acc = jnp.zeros((rc, CONV_CH), F32)
        for a in range(SUBLANE):
            for mm in range(span // SUBLANE):
                kk = mm * SUBLANE + a - shift
                if 0 <= kk < CONV_WIDTH:
                    lo = r * rc + mm * SUBLANE
                    acc = acc + sh_ref[a, lo:lo + rc, :] * w_ref[kk:kk + 1, :]
        y = _layer_norm(acc + cb_ref[...], g_ref[...], b_ref[...])
        o_ref[r * rc:(r + 1) * rc, :] = (y * _sigmoid(y)).astype(o_ref.dtype)


def _conv(zc, lw):
    B, L, _ = zc.shape
    TC = min(512, L)
    nt = L // TC
    hb = TC // CONV_HALO
    nhb = L // CONV_HALO
    const2 = lambda b, j: (0, 0)
    return pl.pallas_call(
        functools.partial(_conv_kernel, tc=TC, nt=nt, rc=min(64, TC)),
        grid=(B, nt),
        in_specs=[
            pl.BlockSpec((None, TC, 2 * CONV_CH), lambda b, j: (b, j, 0)),
            pl.BlockSpec((None, CONV_HALO, 2 * CONV_CH), lambda b, j: (b, jnp.maximum(j * hb - 1, 0), 0)),
            pl.BlockSpec((None, CONV_HALO, 2 * CONV_CH), lambda b, j: (b, jnp.minimum((j + 1) * hb, nhb - 1), 0)),
            pl.BlockSpec((CONV_WIDTH, CONV_CH), const2),
            pl.BlockSpec((1, CONV_CH), const2),
            pl.BlockSpec((1, CONV_CH), const2),
            pl.BlockSpec((1, CONV_CH), const2),
        ],
        out_specs=pl.BlockSpec((None, TC, CONV_CH), lambda b, j: (b, j, 0)),
        out_shape=jax.ShapeDtypeStruct((B, L, CONV_CH), BF16),
        scratch_shapes=[pltpu.VMEM((TC + 2 * CONV_HALO, CONV_CH), F32),
                        pltpu.VMEM((SUBLANE, TC + 2 * CONV_HALO - SUBLANE, CONV_CH), F32)],
        compiler_params=_cparams(("parallel", "parallel")), name="conv",
    )(zc, zc, zc, lw["conv_w"], lw["conv_b"], lw["conv_ln_g"], lw["conv_ln_b"])


def _ssm_kernel(uf_ref, ub_ref, bm_ref, coef_ref, cm_ref, yf_ref, yb_ref,
                sf_ref, sb_ref, hf_ref, hb_ref, *, nb, steps, lc):
    i = pl.program_id(0)
    S = SSM_LANES

    @pl.when(i == 0)
    def _():
        hf_ref[...] = jnp.zeros(hf_ref.shape, F32)
        hb_ref[...] = jnp.zeros(hb_ref.shape, F32)

    sf_ref[...] = _dot(uf_ref[...].astype(BF16), bm_ref[0])
    sb_ref[...] = _dot(ub_ref[...].astype(BF16), bm_ref[1])

    lo_half = lax.broadcasted_iota(jnp.int32, (SUBLANE, lc), 0) < nb

    for c0 in range(0, S, lc):
        re_l = pl.ds(c0, lc)
        im_l = pl.ds(S + c0, lc)
        cf = [coef_ref[0, k, :, c0:c0 + lc] for k in range(4)]
        cb = [coef_ref[1, k, :, c0:c0 + lc] for k in range(4)]

        def body(t, carry, re_l=re_l, im_l=im_l, cf=cf, cb=cb):
            hfr, hfi, hbr, hbi = carry
            rf = pl.ds(pl.multiple_of(t * SUBLANE, SUBLANE), SUBLANE)
            rb = pl.ds(pl.multiple_of((steps - 1 - t) * SUBLANE, SUBLANE), SUBLANE)
            out = []
            for ref, rows, c, hr, hi, fwd in ((sf_ref, rf, cf, hfr, hfi, True), (sb_ref, rb, cb, hbr, hbi, False)):
                xr = ref[rows, re_l]
                xi = ref[rows, im_l]
                nr = c[0] * hr - c[1] * hi + xr
                ni = c[0] * hi + c[1] * hr + xi
                if nb != SUBLANE:
                    sxr = pltpu.roll(xr, nb, 0)
                    sxi = pltpu.roll(xi, nb, 0)
                    nr = nr + (c[2] * sxr - c[3] * sxi)
                    ni = ni + (c[2] * sxi + c[3] * sxr)
                ref[rows, re_l] = nr
                ref[rows, im_l] = ni
                if nb != SUBLANE:
                    snr = pltpu.roll(nr, nb, 0)
                    sni = pltpu.roll(ni, nb, 0)
                    if fwd:
                        nr = jnp.where(lo_half, snr, nr)
                        ni = jnp.where(lo_half, sni, ni)
                    else:
                        nr = jnp.where(lo_half, nr, snr)
                        ni = jnp.where(lo_half, ni, sni)
                out += [nr, ni]
            return tuple(out)

        init = (hf_ref[:, re_l], hf_ref[:, im_l], hb_ref[:, re_l], hb_ref[:, im_l])
        hfr, hfi, hbr, hbi = lax.fori_loop(0, steps, body, init, unroll=2)
        hf_ref[:, re_l] = hfr
        hf_ref[:, im_l] = hfi
        hb_ref[:, re_l] = hbr
        hb_ref[:, im_l] = hbi

    yf_ref[...] = _dot(sf_ref[...].astype(BF16), cm_ref[0])
    yb_ref[...] = _dot(sb_ref[...].astype(BF16), cm_ref[1])


def _ssm(u_rows, lw, nb):
    N = u_rows.shape[0]
    R = min(1024, N)
    n = N // R
    steps = R // SUBLANE
    S2 = 2 * SSM_LANES
    return pl.pallas_call(
        functools.partial(_ssm_kernel, nb=nb, steps=steps, lc=512),
        grid=(n,),
        in_specs=[
            pl.BlockSpec((R, SSM_CH), lambda i: (i, 0)),
            pl.BlockSpec((R, SSM_CH), lambda i: (n - 1 - i, 0)),
            pl.BlockSpec((2, SSM_CH, S2), lambda i: (0, 0, 0)),
            pl.BlockSpec((2, 4, SUBLANE, SSM_LANES), lambda i: (0, 0, 0, 0)),
            pl.BlockSpec((2, S2, SSM_CH), lambda i: (0, 0, 0)),
        ],
        out_specs=[
            pl.BlockSpec((R, SSM_CH), lambda i: (i, 0)),
            pl.BlockSpec((R, SSM_CH), lambda i: (n - 1 - i, 0)),
        ],
        out_shape=[jax.ShapeDtypeStruct((N, SSM_CH), F32)] * 2,
        scratch_shapes=[pltpu.VMEM((R, S2), F32), pltpu.VMEM((R, S2), F32),
                        pltpu.VMEM((SUBLANE, S2), F32), pltpu.VMEM((SUBLANE, S2), F32)],
        compiler_params=_cparams(("arbitrary",)), name="ssm",
    )(u_rows, u_rows, lw["ssm_bm"], lw["ssm_coef" + str(nb)], lw["ssm_cm"])


def _mix_kernel(h_ref, oc_ref, u_ref, yf_ref, yb_ref, om_ref, d_ref, gw_ref, gb_ref, wo_ref, g_ref, b_ref,
                h1_ref, h1b_ref, h1t_ref):
    y = d_ref[...] * u_ref[...] + yf_ref[...] + yb_ref[...]
    g = _gelu(y)
    o_ssm = g * _sigmoid(_dot(g.astype(BF16), gw_ref[...]) + gb_ref[...])
    mix = (_dot(oc_ref[...], wo_ref[0:256, :]) + _dot(o_ssm.astype(BF16), wo_ref[256:512, :])
           + _dot(om_ref[...], wo_ref[512:1024, :]))
    h1 = _layer_norm(DEEPNORM_ALPHA * h_ref[...] + mix, g_ref[...], b_ref[...])
    h1_ref[...] = h1
    h1b_ref[...] = h1.astype(BF16)
    h1t_ref[...] = h1.T.astype(BF16)


def _mix(h, o_conv, zs, yf, yb, o_mla, lw):
    B, L, D = h.shape
    TM = min(1024, L)
    nt = L // TM
    row = lambda b, j: (b, j, 0)
    tb = lambda b, j: (j, b)
    const2 = lambda b, j: (0, 0)
    return pl.pallas_call(
        _mix_kernel,
        grid=(B, nt),
        in_specs=[
            pl.BlockSpec((None, TM, D), row),
            pl.BlockSpec((None, TM, CONV_CH), row),
            pl.BlockSpec((TM, SSM_CH), tb),
            pl.BlockSpec((TM, SSM_CH), tb),
            pl.BlockSpec((TM, SSM_CH), tb),
            pl.BlockSpec((None, TM, MLA_HEADS * MLA_V), row),
            pl.BlockSpec((1, SSM_CH), const2),
            pl.BlockSpec((SSM_CH, SSM_CH), const2),
            pl.BlockSpec((1, SSM_CH), const2),
            pl.BlockSpec((D, D), const2),
            pl.BlockSpec((1, D), const2),
            pl.BlockSpec((1, D), const2),
        ],
        out_specs=[pl.BlockSpec((None, TM, D), row), pl.BlockSpec((None, TM, D), row),
                   pl.BlockSpec((D, TM), lambda b, j: (0, b * nt + j))],
        out_shape=[jax.ShapeDtypeStruct((B, L, D), F32), jax.ShapeDtypeStruct((B, L, D), BF16),
                   jax.ShapeDtypeStruct((D, B * L), BF16)],
        compiler_params=_cparams(("parallel", "parallel")), name="mix",
    )(h, o_conv, zs, yf, yb, o_mla, lw["ssm_d"], lw["ssm_glu_w"], lw["ssm_glu_b"], lw["w_out"],
      lw["ln1_g"], lw["ln1_b"])


def _top_desc(s, n):
    m = jnp.max(s, axis=0, keepdims=True)
    out = [m]
    for _ in range(n - 1):
        m = jnp.max(jnp.where(s < m, s, -jnp.inf), axis=0, keepdims=True)
        out.append(m)
    return out


def _oddeven_merge_sort_pairs(n):
    pairs, p = [], 1
    while p < n:
        k = p
        while k >= 1:
            for j in range(k % p, n - k, 2 * k):
                for i in range(min(k, n - j - k)):
                    if (i + j) // (2 * p) == (i + j + k) // (2 * p):
                        pairs.append((i + j, i + j + k))
            k //= 2
        p *= 2
    return tuple(pairs)


def _top_desc_keys(s, n):
    groups = s.shape[0] // SUBLANE
    r = [s[SUBLANE * i:SUBLANE * (i + 1), :] for i in range(groups)]
    for a, b in _oddeven_merge_sort_pairs(groups):
        r[a], r[b] = jnp.maximum(r[a], r[b]), jnp.minimum(r[a], r[b])
    ninf = jnp.full(r[0].shape, -jnp.inf, F32)
    out = []
    for k in range(n):
        m = jnp.max(r[0], axis=0, keepdims=True)
        out.append(m)
        if k == n - 1:
            break
        won = r[0] == m
        for i in range(min(groups, n - 1 - k)):
            r[i] = jnp.where(won, r[i + 1] if i + 1 < groups else ninf, r[i])
    return out


def _split_bf16(x):
    hi = x.astype(BF16)
    return hi, (x - hi.astype(F32)).astype(BF16)


def _route_kernel(x_ref, wq_ref, k1_ref, k2_ref, r1_ref, e1_ref, rank2_ref, e2_ref, cand_ref):
    q = _dot(x_ref[...], wq_ref[...])
    k_hl = [_split_bf16(k1_ref[...]), _split_bf16(k2_ref[...])]
    cand_ref[...] = jnp.full(cand_ref.shape, -jnp.inf, F32)
    for h in range(PEER_HEADS):
        s, tops = [], []
        for half in range(2):
            lo = h * 2 * PEER_HALF + half * PEER_HALF
            q_hi, q_lo = _split_bf16(q[:, lo:lo + PEER_HALF])
            k_hi, k_lo = k_hl[half]
            sc = _dot_nt(k_hi, q_hi) + (_dot_nt(k_hi, q_lo) + _dot_nt(k_lo, q_hi))
            s.append(sc)
            tops.append(_top_desc_keys(sc, _CAND_N))
        for r, (a, b) in enumerate(_CAND_PAIRS):
            cand_ref[r:r + 1, :] = tops[0][a] + tops[1][b]
        cs = _top_desc(cand_ref[...], _CAND_N)
        thr = 0.5 * (cs[PEER_TOPK - 1] + cs[PEER_TOPK])
        z = jnp.ones_like(thr)
        for kk in range(1, PEER_TOPK):
            z = z + jnp.exp(cs[kk] - cs[0])
        theta = thr - s[0]
        rank2 = jnp.zeros(s[1].shape, F32)
        r1 = jnp.zeros(s[0].shape, F32)
        for kk in range(PEER_TOPK):
            rank2 = rank2 + jnp.where(s[1] < tops[1][kk], 1.0, 0.0)
            r1 = r1 + jnp.where(tops[1][kk] > theta, 1.0, 0.0)
        e1 = jnp.exp(s[0] - tops[0][0]) / z
        e2 = jnp.exp(s[1] - tops[1][0])
        for c in range(s[0].shape[1] // LANE):
            cols = slice(c * LANE, (c + 1) * LANE)
            r1_ref[h, c] = r1[:, cols]
            e1_ref[h, c] = e1[:, cols]
            for ref, val in ((rank2_ref, rank2), (e2_ref, e2)):
                words = pltpu.bitcast(val[:, cols].astype(BF16), jnp.uint32)
                ref[pl.ds((c * PEER_HEADS + h) * PACKED_KEYS, PACKED_KEYS), :] = words


def _route(xb, lw):
    T, D = xb.shape
    TT = min(256, T)
    aux_f32 = jax.ShapeDtypeStruct((PEER_HEADS, T // LANE, PEER_NKEYS, LANE), F32)
    aux_bf16 = jax.ShapeDtypeStruct((T // LANE * PEER_HEADS * PACKED_KEYS, LANE), jnp.uint32)
    f32_spec = pl.BlockSpec((PEER_HEADS, TT // LANE, PEER_NKEYS, LANE), lambda i: (0, i, 0, 0))
    bf16_spec = pl.BlockSpec((TT // LANE * PEER_HEADS * PACKED_KEYS, LANE), lambda i: (i, 0))
    return pl.pallas_call(
        _route_kernel,
        grid=(T // TT,),
        in_specs=[
            pl.BlockSpec((TT, D), lambda i: (i, 0)),
            pl.BlockSpec((D, 2 * PEER_HEADS * PEER_HALF), lambda i: (0, 0)),
            pl.BlockSpec((PEER_NKEYS, PEER_HALF), lambda i: (0, 0)),
            pl.BlockSpec((PEER_NKEYS, PEER_HALF), lambda i: (0, 0)),
        ],
        out_specs=[f32_spec, f32_spec, bf16_spec, bf16_spec],
        out_shape=[aux_f32, aux_f32, aux_bf16, aux_bf16],
        scratch_shapes=[pltpu.VMEM((_CAND_ROWS, TT), F32)],
        compiler_params=_cparams(("parallel",)), name="route",
    )(xb, lw["peer_w_q"], lw["peer_k1"], lw["peer_k2"])


MXU_TILE = 256


def _peer_kernel(xt_ref, u_ref, vt_ref, r1_ref, e1_ref, rank2_ref, e2_ref, o_ref, act_ref, ag_ref, acc_ref,
                 *, ne, eb, tt):
    e = pl.program_id(1)

    @pl.when(e == 0)
    def _():
        acc_ref[...] = jnp.zeros(acc_ref.shape, F32)

    n1 = eb // PEER_NKEYS
    rows = pl.ds(pl.multiple_of(e * n1, n1), n1)
    per_tile = MXU_TILE // LANE
    for tn in range(tt // MXU_TILE):
        tcols = pl.ds(tn * MXU_TILE, MXU_TILE)
        for tk in range(eb // MXU_TILE):
            trows = pl.ds(tk * MXU_TILE, MXU_TILE)
            act_ref[trows, tcols] = _dot(u_ref[trows, :], xt_ref[:, tcols])
            for c in range(tn * per_tile, (tn + 1) * per_tile):
                cols = pl.ds(c * LANE, LANE)
                rr = [r1_ref[h, c, rows, :] for h in range(PEER_HEADS)]
                ee = [e1_ref[h, c, rows, :] for h in range(PEER_HEADS)]
                for i in range(tk * per_tile, (tk + 1) * per_tile):
                    gate = jnp.zeros((PEER_NKEYS // BF16_ROWS, BF16_ROWS, LANE), BF16)
                    for h in range(PEER_HEADS):
                        r16 = jnp.broadcast_to(rr[h][i:i + 1, :], (BF16_ROWS, LANE)).astype(BF16)
                        e16 = jnp.broadcast_to(ee[h][i:i + 1, :], (BF16_ROWS, LANE)).astype(BF16)
                        tile = pl.ds((c * PEER_HEADS + h) * PACKED_KEYS, PACKED_KEYS)
                        rk = pltpu.bitcast(rank2_ref[tile, :], BF16).reshape(gate.shape)
                        e2 = pltpu.bitcast(e2_ref[tile, :], BF16).reshape(gate.shape)
                        gate = gate + jnp.where(rk < r16[None], e2 * e16[None], jnp.zeros((), BF16))
                    for g in range(PEER_NKEYS // BF16_ROWS):
                        er = pl.ds(i * PEER_NKEYS + g * BF16_ROWS, BF16_ROWS)
                        ag_ref[er, cols] = _gelu(act_ref[er, cols].astype(BF16)) * gate[g]
    acc_ref[...] += _dot(vt_ref[...], ag_ref[...])

    @pl.when(e == ne - 1)
    def _():
        o_ref[...] = acc_ref[...].T


def _peer(xt, aux, lw):
    D, T = xt.shape
    TT = min(512, T)
    EB = 2048
    ne = PEER_EXPERTS // EB
    f32_spec = pl.BlockSpec((PEER_HEADS, TT // LANE, PEER_NKEYS, LANE), lambda i, e: (0, i, 0, 0))
    bf16_spec = pl.BlockSpec((TT // LANE * PEER_HEADS * PACKED_KEYS, LANE), lambda i, e: (i, 0))
    return pl.pallas_call(
        functools.partial(_peer_kernel, ne=ne, eb=EB, tt=TT),
        grid=(T // TT, ne),
        in_specs=[
            pl.BlockSpec((D, TT), lambda i, e: (0, i)),
            pl.BlockSpec((EB, D), lambda i, e: (e, 0)),
            pl.BlockSpec((D, EB), lambda i, e: (0, e)),
            f32_spec, f32_spec, bf16_spec, bf16_spec,
        ],
        out_specs=pl.BlockSpec((TT, D), lambda i, e: (i, 0)),
        out_shape=jax.ShapeDtypeStruct((T, D), F32),
        scratch_shapes=[pltpu.VMEM((EB, TT), F32), pltpu.VMEM((EB, TT), BF16), pltpu.VMEM((D, TT), F32)],
        compiler_params=_cparams(("parallel", "arbitrary")), name="peer",
    )(xt, lw["peer_u"], lw["peer_vt"], *aux)


def _ple_kernel(h1_ref, h1b_ref, pe_ref, p_ref, wg_ref, wp_ref, g_ref, b_ref, o_ref):
    gate = _sigmoid(_dot(h1b_ref[...], wg_ref[...]))
    emb = _dot(p_ref[...].astype(BF16), wp_ref[...])
    r = DEEPNORM_ALPHA * h1_ref[...] + pe_ref[...] + gate * emb
    o_ref[...] = _layer_norm(r, g_ref[...], b_ref[...])


def _ple(h1, h1b, peer_out, p, lw):
    T, D = h1.shape
    TM = min(1024, T)
    row = lambda i: (i, 0)
    const = lambda i: (0, 0)
    return pl.pallas_call(
        _ple_kernel,
        grid=(T // TM,),
        in_specs=[
            pl.BlockSpec((TM, D), row), pl.BlockSpec((TM, D), row), pl.BlockSpec((TM, D), row),
            pl.BlockSpec((TM, PLE_DIM), row),
            pl.BlockSpec((D, D), const), pl.BlockSpec((PLE_DIM, D), const),
            pl.BlockSpec((1, D), const), pl.BlockSpec((1, D), const),
        ],
        out_specs=pl.BlockSpec((TM, D), row),
        out_shape=jax.ShapeDtypeStruct((T, D), F32),
        compiler_params=_cparams(("parallel",)), name="ple",
    )(h1, h1b, peer_out, p, lw["ple_w_g"], lw["ple_w_p"], lw["ln2_g"], lw["ln2_b"])


def _ssm_params(a_re, a_im, b_re, b_im, c_re, c_im, log_dt):
    G, P, C = SSM_GROUPS, SSM_STATE, SSM_GROUP
    ar, ai = a_re.astype(F32), a_im.astype(F32)
    dt = jnp.exp(log_dt.astype(F32))[..., None]
    mag = jnp.exp(dt * ar)
    abr, abi = mag * jnp.cos(dt * ai), mag * jnp.sin(dt * ai)
    den = ar * ar + ai * ai
    qr = ((abr - 1.0) * ar + abi * ai) / den
    qi = (abi * ar - (abr - 1.0) * ai) / den
    br, bi = b_re.astype(F32), b_im.astype(F32)
    bbr = qr[..., None] * br - qi[..., None] * bi
    bbi = qr[..., None] * bi + qi[..., None] * br
    eye = jnp.eye(G, dtype=F32)

    def bdiag_in(m):
        return jnp.einsum("dgpc,gh->dgchp", m, eye).reshape(2, G * C, G * P)

    def bdiag_out(m):
        return jnp.einsum("dgcp,gh->dgphc", m, eye).reshape(2, G * P, G * C)

    bm = jnp.concatenate([bdiag_in(bbr), bdiag_in(bbi)], axis=2).astype(BF16)
    cm = jnp.concatenate([bdiag_out(c_re.astype(F32)), -bdiag_out(c_im.astype(F32))], axis=1).astype(BF16)
    a1 = (abr.reshape(2, G * P), abi.reshape(2, G * P))
    a2 = (a1[0] * a1[0] - a1[1] * a1[1], 2.0 * a1[0] * a1[1])
    zero = (jnp.zeros_like(a1[0]), jnp.zeros_like(a1[0]))
    pick = lambda z, d: (z[0][d:d + 1], z[1][d:d + 1])

    def rows(top, bottom):
        shape = (top[0].shape[0], SUBLANE // 2, G * P)
        return [jnp.concatenate([jnp.broadcast_to(t[:, None, :], shape),
                                 jnp.broadcast_to(b[:, None, :], shape)], axis=1) for t, b in zip(top, bottom)]

    c8 = jnp.stack([*rows(a1, a1), *rows(zero, zero)], axis=1)
    f1, f2 = rows(pick(a1, 0), pick(a2, 0)), rows(pick(zero, 0), pick(a1, 0))
    b1, b2 = rows(pick(a2, 1), pick(a1, 1)), rows(pick(a1, 1), pick(zero, 1))
    c4 = jnp.concatenate([jnp.stack([*f1, *f2], axis=1), jnp.stack([*b1, *b2], axis=1)], axis=0)
    return bm, cm, c8.astype(F32), c4.astype(F32)


def _rope_tables(L):
    pos = jnp.arange(L, dtype=F32)
    inv_freq = ROPE_THETA ** (-jnp.arange(0, MLA_ROPE, 2, dtype=F32) / MLA_ROPE)
    ang = pos[:, None] * inv_freq[None, :]
    cos, sin = jnp.cos(ang), jnp.sin(ang)
    z = jnp.zeros_like(cos)
    return (jnp.concatenate([cos, cos, z, z], axis=1),
            jnp.concatenate([-sin, z, z, z], axis=1),
            jnp.concatenate([z, sin, z, z], axis=1))


def _layer_weights(i, w_in, conv_w, conv_b, conv_ln_g, conv_ln_b,
                   ssm_a_re, ssm_a_im, ssm_b_re, ssm_b_im, ssm_c_re, ssm_c_im, ssm_log_dt, ssm_d, ssm_glu_w,
                   ssm_glu_b, mla_q_norm_g, mla_w_uq, mla_kv_norm_g, mla_w_ukv, w_out, ln1_g, ln1_b,
                   peer_w_q, peer_k1, peer_k2, peer_u, peer_v, ple_w_p, ple_w_g, ln2_g, ln2_b):
    row = lambda v: v[i].reshape(1, -1).astype(F32)
    H = MLA_HEADS
    wuq = mla_w_uq[i].reshape(MLA_Q_RANK, H, MLA_DK)
    wuq = jnp.pad(wuq, ((0, 0), (0, 0), (0, MLA_DKP - MLA_DK))).reshape(MLA_Q_RANK, H * MLA_DKP)
    wukv = mla_w_ukv[i].reshape(MLA_KV_RANK, H, MLA_NOPE + MLA_V)
    wukv = jnp.concatenate([wukv[:, :, :MLA_NOPE].reshape(MLA_KV_RANK, H * MLA_NOPE),
                            wukv[:, :, MLA_NOPE:].reshape(MLA_KV_RANK, H * MLA_V)], axis=1)
    bm, cm, c8, c4 = _ssm_params(ssm_a_re[i], ssm_a_im[i], ssm_b_re[i], ssm_b_im[i], ssm_c_re[i], ssm_c_im[i],
                                 ssm_log_dt[i])
    return {
        "w_in": jnp.pad(w_in[i], ((0, 0), (0, IN_PAD - w_in.shape[2]))).astype(BF16),
        "q_norm_g": row(mla_q_norm_g), "w_uq": wuq.astype(BF16),
        "kv_norm_g": row(mla_kv_norm_g), "w_ukv": wukv.astype(BF16),
        "conv_w": conv_w[i].astype(F32), "conv_b": row(conv_b),
        "conv_ln_g": row(conv_ln_g), "conv_ln_b": row(conv_ln_b),
        "ssm_bm": bm, "ssm_cm": cm, "ssm_coef8": c8, "ssm_coef4": c4,
        "ssm_d": row(ssm_d), "ssm_glu_w": ssm_glu_w[i].astype(BF16), "ssm_glu_b": row(ssm_glu_b),
        "w_out": w_out[i].astype(BF16), "ln1_g": row(ln1_g), "ln1_b": row(ln1_b),
        "peer_w_q": peer_w_q[i].astype(BF16), "peer_k1": peer_k1[i].astype(F32), "peer_k2": peer_k2[i].astype(F32),
        "peer_u": peer_u[i].astype(BF16), "peer_vt": peer_v[i].T.astype(BF16),
        "ple_w_p": ple_w_p[i].astype(BF16), "ple_w_g": ple_w_g[i].astype(BF16),
        "ln2_g": row(ln2_g), "ln2_b": row(ln2_b),
    }


def _run(x, p, layers, ln_emb):
    B, L, D = x.shape
    assert B in (SUBLANE // 2, SUBLANE), "the S5 scan packs 4 or 8 batch rows per sublane group"
    rope_tabs = _rope_tables(L)
    h = x
    for i, lw in enumerate(layers):
        if i == 0:
            h, zc, zs, q, k, v = _in_proj(h, lw, rope_tabs, ln=ln_emb)
        else:
            zc, zs, q, k, v = _in_proj(h, lw, rope_tabs)
        o_mla = _attention(q, k, v)
        o_conv = _conv(zc, lw)
        yf, yb = _ssm(zs.reshape(L * B, SSM_CH), lw, B)
        h1, h1b, h1t = _mix(h, o_conv, zs, yf.reshape(L, B * SSM_CH), yb.reshape(L, B * SSM_CH), o_mla, lw)
        h1 = h1.reshape(B * L, D)
        h1b = h1b.reshape(B * L, D)
        aux = _route(h1b, lw)
        peer_out = _peer(h1t, aux, lw)
        h = _ple(h1, h1b, peer_out, p[i].reshape(B * L, PLE_DIM), lw).reshape(B, L, D)
    return h


def kernel(x_prompt, x_sample, p_prompt, p_sample, ln_emb_g, ln_emb_b, w_in, conv_w, conv_b, conv_ln_g, conv_ln_b, ssm_a_re, ssm_a_im, ssm_b_re, ssm_b_im, ssm_c_re, ssm_c_im, ssm_log_dt, ssm_d, ssm_glu_w, ssm_glu_b, mla_q_norm_g, mla_w_uq, mla_kv_norm_g, mla_w_ukv, w_out, ln1_g, ln1_b, peer_w_q, peer_k1, peer_k2, peer_u, peer_v, ple_w_p, ple_w_g, ln2_g, ln2_b):
    layers = [
        _layer_weights(i, w_in, conv_w, conv_b, conv_ln_g, conv_ln_b,
                       ssm_a_re, ssm_a_im, ssm_b_re, ssm_b_im, ssm_c_re, ssm_c_im, ssm_log_dt, ssm_d, ssm_glu_w,
                       ssm_glu_b, mla_q_norm_g, mla_w_uq, mla_kv_norm_g, mla_w_ukv, w_out, ln1_g, ln1_b,
                       peer_w_q, peer_k1, peer_k2, peer_u, peer_v, ple_w_p, ple_w_g, ln2_g, ln2_b)
        for i in range(DEPTH)
    ]
    ln_emb = (ln_emb_g.reshape(1, -1).astype(F32), ln_emb_b.reshape(1, -1).astype(F32))
    y_prompt = _run(x_prompt, p_prompt, layers, ln_emb)
    y_sample = _run(x_sample, p_sample, layers, ln_emb)
    return (y_prompt, y_sample)
```

```python
import functools
import math

import jax
import jax.numpy as jnp
from jax import lax
from jax.experimental import pallas as pl
from jax.experimental.pallas import tpu as pltpu

F32 = jnp.float32
BF16 = jnp.bfloat16

D_MODEL = 1024
DEPTH = 2
CONV_CH = 256
CONV_WIDTH = 31
CONV_HALO = 16
SSM_CH = 256
SSM_GROUP = 16
SSM_GROUPS = 16
SSM_STATE = 64
SSM_LANES = SSM_GROUPS * SSM_STATE
MLA_HEADS = 4
MLA_NOPE = 128
MLA_ROPE = 64
MLA_V = 128
MLA_DK = MLA_NOPE + MLA_ROPE
MLA_DKP = 256
MLA_Q_RANK = 512
MLA_KV_RANK = 256
ROPE_THETA = 10000.0
PEER_HEADS = 8
PEER_NKEYS = 128
PEER_EXPERTS = PEER_NKEYS * PEER_NKEYS
PEER_HALF = 128
PEER_TOPK = 16
PLE_DIM = 256
DEEPNORM_ALPHA = float((2 * DEPTH) ** 0.25)
LN_EPS = 1e-5
RMS_EPS = 1e-6
LANE = 128
SUBLANE = 8
BF16_ROWS = 2 * SUBLANE
PACKED_KEYS = PEER_NKEYS // 2
VMEM_LIMIT = 56 * 1024 * 1024
IN_PAD = 2 * CONV_CH + SSM_CH + MLA_Q_RANK + MLA_KV_RANK + LANE

_CAND_N = PEER_TOPK + 1
_CAND_PAIRS = tuple((i, j) for i in range(_CAND_N) for j in range(_CAND_N) if (i + 1) * (j + 1) <= _CAND_N)
_CAND_ROWS = -(-len(_CAND_PAIRS) // SUBLANE) * SUBLANE


def _cparams(sem):
    return pltpu.CompilerParams(dimension_semantics=sem, vmem_limit_bytes=VMEM_LIMIT)


def _sigmoid(x):
    return 1.0 / (1.0 + jnp.exp(-x))


def _gelu(x):
    return 0.5 * x * (1.0 + jnp.tanh(math.sqrt(2.0 / math.pi) * (x + 0.044715 * (x * x * x))))


def _layer_norm(x, g, b):
    mu = jnp.mean(x, axis=-1, keepdims=True)
    xc = x - mu
    var = jnp.mean(xc * xc, axis=-1, keepdims=True)
    return xc * lax.rsqrt(var + LN_EPS) * g + b


def _rms_norm(x, g):
    return x * lax.rsqrt(jnp.mean(x * x, axis=-1, keepdims=True) + RMS_EPS) * g


def _dot(a, b):
    return jnp.dot(a, b, preferred_element_type=F32)


def _dot_nt(a, b):
    return lax.dot_general(a, b, (((1,), (1,)), ((), ())), preferred_element_type=F32)


def _in_proj_kernel(*refs, pre_ln):
    if pre_ln:
        (x_ref, g_ref, b_ref, w_ref, qg_ref, wuq_ref, kvg_ref, wukv_ref, rc_ref, rs1_ref, rs2_ref,
         h_ref, zc_ref, zs_ref, q_ref, k_ref, v_ref) = refs
    else:
        (x_ref, w_ref, qg_ref, wuq_ref, kvg_ref, wukv_ref, rc_ref, rs1_ref, rs2_ref,
         zc_ref, zs_ref, q_ref, k_ref, v_ref) = refs
    x = x_ref[...]
    if pre_ln:
        x = _layer_norm(x, g_ref[...], b_ref[...])
        h_ref[...] = x
    z = _dot(x.astype(BF16), w_ref[...])
    zc_ref[...] = z[:, 0:512]
    zs_ref[...] = z[:, 512:768]
    cq = _rms_norm(z[:, 768:1280], qg_ref[...])
    ckv = _rms_norm(z[:, 1280:1536], kvg_ref[...])
    kr = z[:, 1536:1664]
    q = _dot(cq.astype(BF16), wuq_ref[...])
    kv = _dot(ckv.astype(BF16), wukv_ref[...])
    rc, rs1, rs2 = rc_ref[...], rs1_ref[...], rs2_ref[...]

    def rope(t):
        return t * rc + pltpu.roll(t, 96, 1) * rs1 + pltpu.roll(t, 32, 1) * rs2

    scale = MLA_DK ** -0.5 * math.log2(math.e)
    k_rope = rope(kr).astype(BF16)
    for h in range(MLA_HEADS):
        base = h * MLA_DKP
        q_ref[h, :, 0:128] = (q[:, base:base + 128] * scale).astype(BF16)
        q_ref[h, :, 128:256] = (rope(q[:, base + 128:base + 256]) * scale).astype(BF16)
        k_ref[h, :, 0:128] = kv[:, h * 128:(h + 1) * 128].astype(BF16)
        k_ref[h, :, 128:256] = k_rope
        v_ref[h] = kv[:, 512 + h * 128:512 + (h + 1) * 128].astype(BF16)


def _in_proj(x, lw, rope_tabs, ln=None):
    B, L, D = x.shape
    TM = min(1024, L)
    nt = L // TM
    pre_ln = ln is not None
    row = lambda b, j: (b, j, 0)
    const2 = lambda b, j: (0, 0)
    in_specs = [pl.BlockSpec((None, TM, D), row)]
    args = [x]
    if pre_ln:
        in_specs += [pl.BlockSpec((1, D), const2)] * 2
        args += [ln[0], ln[1]]
    in_specs += [
        pl.BlockSpec((D, IN_PAD), const2),
        pl.BlockSpec((1, MLA_Q_RANK), const2),
        pl.BlockSpec((MLA_Q_RANK, MLA_HEADS * MLA_DKP), const2),
        pl.BlockSpec((1, MLA_KV_RANK), const2),
        pl.BlockSpec((MLA_KV_RANK, 2 * MLA_HEADS * 128), const2),
    ] + [pl.BlockSpec((TM, LANE), lambda b, j: (j, 0))] * 3
    args += [lw["w_in"], lw["q_norm_g"], lw["w_uq"], lw["kv_norm_g"], lw["w_ukv"], *rope_tabs]
    out_shape, out_specs = [], []
    if pre_ln:
        out_shape.append(jax.ShapeDtypeStruct((B, L, D), F32))
        out_specs.append(pl.BlockSpec((None, TM, D), row))
    out_shape += [
        jax.ShapeDtypeStruct((B, L, 2 * CONV_CH), F32),
        jax.ShapeDtypeStruct((L, B * SSM_CH), F32),
        jax.ShapeDtypeStruct((B, MLA_HEADS, L, MLA_DKP), BF16),
        jax.ShapeDtypeStruct((B, MLA_HEADS, L, MLA_DKP), BF16),
        jax.ShapeDtypeStruct((B, MLA_HEADS, L, MLA_V), BF16),
    ]
    hrow = lambda b, j: (b, 0, j, 0)
    out_specs += [
        pl.BlockSpec((None, TM, 2 * CONV_CH), row),
        pl.BlockSpec((TM, SSM_CH), lambda b, j: (j, b)),
        pl.BlockSpec((None, MLA_HEADS, TM, MLA_DKP), hrow),
        pl.BlockSpec((None, MLA_HEADS, TM, MLA_DKP), hrow),
        pl.BlockSpec((None, MLA_HEADS, TM, MLA_V), hrow),
    ]
    return pl.pallas_call(
        functools.partial(_in_proj_kernel, pre_ln=pre_ln),
        grid=(B, nt), in_specs=in_specs, out_specs=out_specs, out_shape=out_shape,
        compiler_params=_cparams(("parallel", "parallel")), name="in_proj",
    )(*args)


def _attn_kernel(q_ref, k_ref, v_ref, o_ref, m_ref, l_ref, acc_ref, s_ref, p_ref, *, nk, tk):
    j = pl.program_id(3)

    @pl.when(j == 0)
    def _():
        m_ref[...] = jnp.full(m_ref.shape, -jnp.inf, F32)
        l_ref[...] = jnp.zeros(l_ref.shape, F32)
        acc_ref[...] = jnp.zeros(acc_ref.shape, F32)

    s_ref[...] = _dot_nt(q_ref[...], k_ref[...])
    chunks = [pl.ds(c * LANE, LANE) for c in range(tk // LANE)]
    m_part = s_ref[:, chunks[0]]
    for ch in chunks[1:]:
        m_part = jnp.maximum(m_part, s_ref[:, ch])
    m_prev = m_ref[...]
    m_new = jnp.maximum(m_prev, jnp.max(m_part, axis=1, keepdims=True))
    alpha = jnp.exp2(m_prev - m_new)
    l_part = jnp.zeros(m_new.shape, F32)
    for ch in chunks:
        p = jnp.exp2(s_ref[:, ch] - m_new)
        p_ref[:, ch] = p.astype(BF16)
        l_part = l_part + p
    l_ref[...] = alpha * l_ref[...] + jnp.sum(l_part, axis=1, keepdims=True)
    acc_ref[...] = alpha * acc_ref[...] + _dot(p_ref[...], v_ref[...])
    m_ref[...] = m_new

    @pl.when(j == nk - 1)
    def _():
        o_ref[...] = (acc_ref[...] / l_ref[...]).astype(o_ref.dtype)


def _attention(q, k, v):
    B, H, L, _ = q.shape
    TQ = min(2048, L)
    TK = min(2048, L)
    nq, nk = L // TQ, L // TK
    return pl.pallas_call(
        functools.partial(_attn_kernel, nk=nk, tk=TK),
        grid=(B, H, nq, nk),
        in_specs=[
            pl.BlockSpec((None, None, TQ, MLA_DKP), lambda b, h, i, j: (b, h, i, 0)),
            pl.BlockSpec((None, None, TK, MLA_DKP), lambda b, h, i, j: (b, h, j, 0)),
            pl.BlockSpec((None, None, TK, MLA_V), lambda b, h, i, j: (b, h, j, 0)),
        ],
        out_specs=pl.BlockSpec((None, TQ, MLA_V), lambda b, h, i, j: (b, i, h)),
        out_shape=jax.ShapeDtypeStruct((B, L, H * MLA_V), BF16),
        scratch_shapes=[pltpu.VMEM((TQ, LANE), F32), pltpu.VMEM((TQ, LANE), F32),
                        pltpu.VMEM((TQ, MLA_V), F32), pltpu.VMEM((TQ, TK), F32), pltpu.VMEM((TQ, TK), BF16)],
        compiler_params=_cparams(("parallel", "parallel", "parallel", "arbitrary")), name="attn",
    )(q, k, v)


def _conv_kernel(zm_ref, zp_ref, zn_ref, w_ref, cb_ref, g_ref, b_ref, o_ref, hp_ref, sh_ref, *, tc, nt, rc):
    j = pl.program_id(1)

    def glu(z):
        return z[:, :CONV_CH] * _sigmoid(z[:, CONV_CH:])

    hp_ref[0:CONV_HALO, :] = jnp.where(j > 0, glu(zp_ref[...]), 0.0)
    hp_ref[CONV_HALO:CONV_HALO + tc, :] = glu(zm_ref[...])
    hp_ref[CONV_HALO + tc:2 * CONV_HALO + tc, :] = jnp.where(j < nt - 1, glu(zn_ref[...]), 0.0)
    shift = CONV_HALO - CONV_WIDTH // 2
    span = -(-(CONV_WIDTH + shift) // SUBLANE) * SUBLANE
    for a in range(SUBLANE):
        sh_ref[a] = hp_ref[a:a + tc + span - SUBLANE, :]
    for r in range(tc // rc):
        acc = jnp.zeros((rc, CONV_CH), F32)
        for a in range(SUBLANE):
            for mm in range(span // SUBLANE):
                kk = mm * SUBLANE + a - shift
                if 0 <= kk < CONV_WIDTH:
                    lo = r * rc + mm * SUBLANE
                    acc = acc + sh_ref[a, lo:lo + rc, :] * w_ref[kk:kk + 1, :]
        y = _layer_norm(acc + cb_ref[...], g_ref[...], b_ref[...])
        o_ref[r * rc:(r + 1) * rc, :] = (y * _sigmoid(y)).astype(o_ref.dtype)


def _conv(zc, lw):
    B, L, _ = zc.shape
    TC = min(512, L)
    nt = L // TC
    hb = TC // CONV_HALO
    nhb = L // CONV_HALO
    const2 = lambda b, j: (0, 0)
    return pl.pallas_call(
        functools.partial(_conv_kernel, tc=TC, nt=nt, rc=min(64, TC)),
        grid=(B, nt),
        in_specs=[
            pl.BlockSpec((None, TC, 2 * CONV_CH), lambda b, j: (b, j, 0)),
            pl.BlockSpec((None, CONV_HALO, 2 * CONV_CH), lambda b, j: (b, jnp.maximum(j * hb - 1, 0), 0)),
            pl.BlockSpec((None, CONV_HALO, 2 * CONV_CH), lambda b, j: (b, jnp.minimum((j + 1) * hb, nhb - 1), 0)),
            pl.BlockSpec((CONV_WIDTH, CONV_CH), const2),
            pl.BlockSpec((1, CONV_CH), const2),
            pl.BlockSpec((1, CONV_CH), const2),
            pl.BlockSpec((1, CONV_CH), const2),
        ],
        out_specs=pl.BlockSpec((None, TC, CONV_CH), lambda b, j: (b, j, 0)),
        out_shape=jax.ShapeDtypeStruct((B, L, CONV_CH), BF16),
        scratch_shapes=[pltpu.VMEM((TC + 2 * CONV_HALO, CONV_CH), F32),
                        pltpu.VMEM((SUBLANE, TC + 2 * CONV_HALO - SUBLANE, CONV_CH), F32)],
        compiler_params=_cparams(("parallel", "parallel")), name="conv",
    )(zc, zc, zc, lw["conv_w"], lw["conv_b"], lw["conv_ln_g"], lw["conv_ln_b"])


def _ssm_kernel(uf_ref, ub_ref, bm_ref, coef_ref, cm_ref, yf_ref, yb_ref,
                sf_ref, sb_ref, hf_ref, hb_ref, *, nb, steps, lc):
    i = pl.program_id(0)
    S = SSM_LANES

    @pl.when(i == 0)
    def _():
        hf_ref[...] = jnp.zeros(hf_ref.shape, F32)
        hb_ref[...] = jnp.zeros(hb_ref.shape, F32)

    sf_ref[...] = _dot(uf_ref[...].astype(BF16), bm_ref[0])
    sb_ref[...] = _dot(ub_ref[...].astype(BF16), bm_ref[1])

    lo_half = lax.broadcasted_iota(jnp.int32, (SUBLANE, lc), 0) < nb

    for c0 in range(0, S, lc):
        re_l = pl.ds(c0, lc)
        im_l = pl.ds(S + c0, lc)
        cf = [coef_ref[0, k, :, c0:c0 + lc] for k in range(4)]
        cb = [coef_ref[1, k, :, c0:c0 + lc] for k in range(4)]

        def body(t, carry, re_l=re_l, im_l=im_l, cf=cf, cb=cb):
            hfr, hfi, hbr, hbi = carry
            rf = pl.ds(pl.multiple_of(t * SUBLANE, SUBLANE), SUBLANE)
            rb = pl.ds(pl.multiple_of((steps - 1 - t) * SUBLANE, SUBLANE), SUBLANE)
            out = []
            for ref, rows, c, hr, hi, fwd in ((sf_ref, rf, cf, hfr, hfi, True), (sb_ref, rb, cb, hbr, hbi, False)):
                xr = ref[rows, re_l]
                xi = ref[rows, im_l]
                nr = c[0] * hr - c[1] * hi + xr
                ni = c[0] * hi + c[1] * hr + xi
                if nb != SUBLANE:
                    sxr = pltpu.roll(xr, nb, 0)
                    sxi = pltpu.roll(xi, nb, 0)
                    nr = nr + (c[2] * sxr - c[3] * sxi)
                    ni = ni + (c[2] * sxi + c[3] * sxr)
                ref[rows, re_l] = nr
                ref[rows, im_l] = ni
                if nb != SUBLANE:
                    snr = pltpu.roll(nr, nb, 0)
                    sni = pltpu.roll(ni, nb, 0)
                    if fwd:
                        nr = jnp.where(lo_half, snr, nr)
                        ni = jnp.where(lo_half, sni, ni)
                    else:
                        nr = jnp.where(lo_half, nr, snr)
                        ni = jnp.where(lo_half, ni, sni)
                out += [nr, ni]
            return tuple(out)

        init = (hf_ref[:, re_l], hf_ref[:, im_l], hb_ref[:, re_l], hb_ref[:, im_l])
        hfr, hfi, hbr, hbi = lax.fori_loop(0, steps, body, init, unroll=4)
        hf_ref[:, re_l] = hfr
        hf_ref[:, im_l] = hfi
        hb_ref[:, re_l] = hbr
        hb_ref[:, im_l] = hbi

    yf_ref[...] = _dot(sf_ref[...].astype(BF16), cm_ref[0])
    yb_ref[...] = _dot(sb_ref[...].astype(BF16), cm_ref[1])


def _ssm(u_rows, lw, nb):
    N = u_rows.shape[0]
    R = min(1024, N)
    n = N // R
    steps = R // SUBLANE
    S2 = 2 * SSM_LANES
    return pl.pallas_call(
        functools.partial(_ssm_kernel, nb=nb, steps=steps, lc=512),
        grid=(n,),
        in_specs=[
            pl.BlockSpec((R, SSM_CH), lambda i: (i, 0)),
            pl.BlockSpec((R, SSM_CH), lambda i: (n - 1 - i, 0)),
            pl.BlockSpec((2, SSM_CH, S2), lambda i: (0, 0, 0)),
            pl.BlockSpec((2, 4, SUBLANE, SSM_LANES), lambda i: (0, 0, 0, 0)),
            pl.BlockSpec((2, S2, SSM_CH), lambda i: (0, 0, 0)),
        ],
        out_specs=[
            pl.BlockSpec((R, SSM_CH), lambda i: (i, 0)),
            pl.BlockSpec((R, SSM_CH), lambda i: (n - 1 - i, 0)),
        ],
        out_shape=[jax.ShapeDtypeStruct((N, SSM_CH), F32)] * 2,
        scratch_shapes=[pltpu.VMEM((R, S2), F32), pltpu.VMEM((R, S2), F32),
                        pltpu.VMEM((SUBLANE, S2), F32), pltpu.VMEM((SUBLANE, S2), F32)],
        compiler_params=_cparams(("arbitrary",)), name="ssm",
    )(u_rows, u_rows, lw["ssm_bm"], lw["ssm_coef" + str(nb)], lw["ssm_cm"])


def _mix_kernel(h_ref, oc_ref, u_ref, yf_ref, yb_ref, om_ref, d_ref, gw_ref, gb_ref, wo_ref, g_ref, b_ref,
                h1_ref, h1b_ref, h1t_ref):
    y = d_ref[...] * u_ref[...] + yf_ref[...] + yb_ref[...]
    g = _gelu(y)
    o_ssm = g * _sigmoid(_dot(g.astype(BF16), gw_ref[...]) + gb_ref[...])
    mix = (_dot(oc_ref[...], wo_ref[0:256, :]) + _dot(o_ssm.astype(BF16), wo_ref[256:512, :])
           + _dot(om_ref[...], wo_ref[512:1024, :]))
    h1 = _layer_norm(DEEPNORM_ALPHA * h_ref[...] + mix, g_ref[...], b_ref[...])
    h1_ref[...] = h1
    h1b_ref[...] = h1.astype(BF16)
    h1t_ref[...] = h1.T.astype(BF16)


def _mix(h, o_conv, zs, yf, yb, o_mla, lw):
    B, L, D = h.shape
    TM = min(1024, L)
    nt = L // TM
    row = lambda b, j: (b, j, 0)
    tb = lambda b, j: (j, b)
    const2 = lambda b, j: (0, 0)
    return pl.pallas_call(
        _mix_kernel,
        grid=(B, nt),
        in_specs=[
            pl.BlockSpec((None, TM, D), row),
            pl.BlockSpec((None, TM, CONV_CH), row),
            pl.BlockSpec((TM, SSM_CH), tb),
            pl.BlockSpec((TM, SSM_CH), tb),
            pl.BlockSpec((TM, SSM_CH), tb),
            pl.BlockSpec((None, TM, MLA_HEADS * MLA_V), row),
            pl.BlockSpec((1, SSM_CH), const2),
            pl.BlockSpec((SSM_CH, SSM_CH), const2),
            pl.BlockSpec((1, SSM_CH), const2),
            pl.BlockSpec((D, D), const2),
            pl.BlockSpec((1, D), const2),
            pl.BlockSpec((1, D), const2),
        ],
        out_specs=[pl.BlockSpec((None, TM, D), row), pl.BlockSpec((None, TM, D), row),
                   pl.BlockSpec((D, TM), lambda b, j: (0, b * nt + j))],
        out_shape=[jax.ShapeDtypeStruct((B, L, D), F32), jax.ShapeDtypeStruct((B, L, D), BF16),
                   jax.ShapeDtypeStruct((D, B * L), BF16)],
        compiler_params=_cparams(("parallel", "parallel")), name="mix",
    )(h, o_conv, zs, yf, yb, o_mla, lw["ssm_d"], lw["ssm_glu_w"], lw["ssm_glu_b"], lw["w_out"],
      lw["ln1_g"], lw["ln1_b"])


def _top_desc(s, n):
    m = jnp.max(s, axis=0, keepdims=True)
    out = [m]
    for _ in range(n - 1):
        m = jnp.max(jnp.where(s < m, s, -jnp.inf), axis=0, keepdims=True)
        out.append(m)
    return out


def _oddeven_merge_sort_pairs(n):
    pairs, p = [], 1
    while p < n:
        k = p
        while k >= 1:
            for j in range(k % p, n - k, 2 * k):
                for i in range(min(k, n - j - k)):
                    if (i + j) // (2 * p) == (i + j + k) // (2 * p):
                        pairs.append((i + j, i + j + k))
            k //= 2
        p *= 2
    return tuple(pairs)


def _top_desc_keys(s, n):
    groups = s.shape[0] // SUBLANE
    r = [s[SUBLANE * i:SUBLANE * (i + 1), :] for i in range(groups)]
    for a, b in _oddeven_merge_sort_pairs(groups):
        r[a], r[b] = jnp.maximum(r[a], r[b]), jnp.minimum(r[a], r[b])
    ninf = jnp.full(r[0].shape, -jnp.inf, F32)
    out = []
    for k in range(n):
        m = jnp.max(r[0], axis=0, keepdims=True)
        out.append(m)
        if k == n - 1:
            break
        won = r[0] == m
        for i in range(min(groups, n - 1 - k)):
            r[i] = jnp.where(won, r[i + 1] if i + 1 < groups else ninf, r[i])
    return out


def _split_bf16(x):
    hi = x.astype(BF16)
    return hi, (x - hi.astype(F32)).astype(BF16)


def _route_kernel(x_ref, wq_ref, k1_ref, k2_ref, r1_ref, e1_ref, rank2_ref, e2_ref, cand_ref):
    q = _dot(x_ref[...], wq_ref[...])
    k_hl = [_split_bf16(k1_ref[...]), _split_bf16(k2_ref[...])]
    cand_ref[...] = jnp.full(cand_ref.shape, -jnp.inf, F32)
    for h in range(PEER_HEADS):
        s, tops = [], []
        for half in range(2):
            lo = h * 2 * PEER_HALF + half * PEER_HALF
            q_hi, q_lo = _split_bf16(q[:, lo:lo + PEER_HALF])
            k_hi, k_lo = k_hl[half]
            sc = _dot_nt(k_hi, q_hi) + (_dot_nt(k_hi, q_lo) + _dot_nt(k_lo, q_hi))
            s.append(sc)
            tops.append(_top_desc_keys(sc, _CAND_N))
        for r, (a, b) in enumerate(_CAND_PAIRS):
            cand_ref[r:r + 1, :] = tops[0][a] + tops[1][b]
        cs = _top_desc(cand_ref[...], _CAND_N)
        thr = 0.5 * (cs[PEER_TOPK - 1] + cs[PEER_TOPK])
        z = jnp.ones_like(thr)
        for kk in range(1, PEER_TOPK):
            z = z + jnp.exp(cs[kk] - cs[0])
        theta = thr - s[0]
        rank2 = jnp.zeros(s[1].shape, F32)
        r1 = jnp.zeros(s[0].shape, F32)
        for kk in range(PEER_TOPK):
            rank2 = rank2 + jnp.where(s[1] < tops[1][kk], 1.0, 0.0)
            r1 = r1 + jnp.where(tops[1][kk] > theta, 1.0, 0.0)
        e1 = jnp.exp(s[0] - tops[0][0]) / z
        e2 = jnp.exp(s[1] - tops[1][0])
        for c in range(s[0].shape[1] // LANE):
            cols = slice(c * LANE, (c + 1) * LANE)
            r1_ref[h, c] = r1[:, cols]
            e1_ref[h, c] = e1[:, cols]
            for ref, val in ((rank2_ref, rank2), (e2_ref, e2)):
                words = pltpu.bitcast(val[:, cols].astype(BF16), jnp.uint32)
                ref[pl.ds((c * PEER_HEADS + h) * PACKED_KEYS, PACKED_KEYS), :] = words


def _route(xb, lw):
    T, D = xb.shape
    TT = min(256, T)
    aux_f32 = jax.ShapeDtypeStruct((PEER_HEADS, T // LANE, PEER_NKEYS, LANE), F32)
    aux_bf16 = jax.ShapeDtypeStruct((T // LANE * PEER_HEADS * PACKED_KEYS, LANE), jnp.uint32)
    f32_spec = pl.BlockSpec((PEER_HEADS, TT // LANE, PEER_NKEYS, LANE), lambda i: (0, i, 0, 0))
    bf16_spec = pl.BlockSpec((TT // LANE * PEER_HEADS * PACKED_KEYS, LANE), lambda i: (i, 0))
    return pl.pallas_call(
        _route_kernel,
        grid=(T // TT,),
        in_specs=[
            pl.BlockSpec((TT, D), lambda i: (i, 0)),
            pl.BlockSpec((D, 2 * PEER_HEADS * PEER_HALF), lambda i: (0, 0)),
            pl.BlockSpec((PEER_NKEYS, PEER_HALF), lambda i: (0, 0)),
            pl.BlockSpec((PEER_NKEYS, PEER_HALF), lambda i: (0, 0)),
        ],
        out_specs=[f32_spec, f32_spec, bf16_spec, bf16_spec],
        out_shape=[aux_f32, aux_f32, aux_bf16, aux_bf16],
        scratch_shapes=[pltpu.VMEM((_CAND_ROWS, TT), F32)],
        compiler_params=_cparams(("parallel",)), name="route",
    )(xb, lw["peer_w_q"], lw["peer_k1"], lw["peer_k2"])


MXU_TILE = 256


def _peer_kernel(xt_ref, u_ref, vt_ref, r1_ref, e1_ref, rank2_ref, e2_ref, o_ref, act_ref, ag_ref, acc_ref,
                 *, ne, eb, tt):
    e = pl.program_id(1)

    @pl.when(e == 0)
    def _():
        acc_ref[...] = jnp.zeros(acc_ref.shape, F32)

    n1 = eb // PEER_NKEYS
    rows = pl.ds(pl.multiple_of(e * n1, n1), n1)
    per_tile = MXU_TILE // LANE
    for tn in range(tt // MXU_TILE):
        tcols = pl.ds(tn * MXU_TILE, MXU_TILE)
        for tk in range(eb // MXU_TILE):
            trows = pl.ds(tk * MXU_TILE, MXU_TILE)
            act_ref[trows, tcols] = _dot(u_ref[trows, :], xt_ref[:, tcols])
            for c in range(tn * per_tile, (tn + 1) * per_tile):
                cols = pl.ds(c * LANE, LANE)
                rr = [r1_ref[h, c, rows, :] for h in range(PEER_HEADS)]
                ee = [e1_ref[h, c, rows, :] for h in range(PEER_HEADS)]
                for i in range(tk * per_tile, (tk + 1) * per_tile):
                    gate = jnp.zeros((PEER_NKEYS // BF16_ROWS, BF16_ROWS, LANE), BF16)
                    for h in range(PEER_HEADS):
                        r16 = jnp.broadcast_to(rr[h][i:i + 1, :], (BF16_ROWS, LANE)).astype(BF16)
                        e16 = jnp.broadcast_to(ee[h][i:i + 1, :], (BF16_ROWS, LANE)).astype(BF16)
                        tile = pl.ds((c * PEER_HEADS + h) * PACKED_KEYS, PACKED_KEYS)
                        rk = pltpu.bitcast(rank2_ref[tile, :], BF16).reshape(gate.shape)
                        e2 = pltpu.bitcast(e2_ref[tile, :], BF16).reshape(gate.shape)
                        gate = gate + jnp.where(rk < r16[None], e2 * e16[None], jnp.zeros((), BF16))
                    for g in range(PEER_NKEYS // BF16_ROWS):
                        er = pl.ds(i * PEER_NKEYS + g * BF16_ROWS, BF16_ROWS)
                        ag_ref[er, cols] = _gelu(act_ref[er, cols].astype(BF16)) * gate[g]
    acc_ref[...] += _dot(vt_ref[...], ag_ref[...])

    @pl.when(e == ne - 1)
    def _():
        o_ref[...] = acc_ref[...].T


def _peer(xt, aux, lw):
    D, T = xt.shape
    TT = min(512, T)
    EB = 2048
    ne = PEER_EXPERTS // EB
    f32_spec = pl.BlockSpec((PEER_HEADS, TT // LANE, PEER_NKEYS, LANE), lambda i, e: (0, i, 0, 0))
    bf16_spec = pl.BlockSpec((TT // LANE * PEER_HEADS * PACKED_KEYS, LANE), lambda i, e: (i, 0))
    return pl.pallas_call(
        functools.partial(_peer_kernel, ne=ne, eb=EB, tt=TT),
        grid=(T // TT, ne),
        in_specs=[
            pl.BlockSpec((D, TT), lambda i, e: (0, i)),
            pl.BlockSpec((EB, D), lambda i, e: (e, 0)),
            pl.BlockSpec((D, EB), lambda i, e: (0, e)),
            f32_spec, f32_spec, bf16_spec, bf16_spec,
        ],
        out_specs=pl.BlockSpec((TT, D), lambda i, e: (i, 0)),
        out_shape=jax.ShapeDtypeStruct((T, D), F32),
        scratch_shapes=[pltpu.VMEM((EB, TT), F32), pltpu.VMEM((EB, TT), BF16), pltpu.VMEM((D, TT), F32)],
        compiler_params=_cparams(("parallel", "arbitrary")), name="peer",
    )(xt, lw["peer_u"], lw["peer_vt"], *aux)


def _ple_kernel(h1_ref, pe_ref, p_ref, wg_ref, wp_ref, g_ref, b_ref, o_ref):
    h1 = h1_ref[...]
    gate = _sigmoid(_dot(h1.astype(BF16), wg_ref[...]))
    emb = _dot(p_ref[...].astype(BF16), wp_ref[...])
    r = DEEPNORM_ALPHA * h1 + pe_ref[...] + gate * emb
    o_ref[...] = _layer_norm(r, g_ref[...], b_ref[...])


def _ple(h1, peer_out, p, lw):
    T, D = h1.shape
    TM = min(1024, T)
    row = lambda i: (i, 0)
    const = lambda i: (0, 0)
    return pl.pallas_call(
        _ple_kernel,
        grid=(T // TM,),
        in_specs=[
            pl.BlockSpec((TM, D), row), pl.BlockSpec((TM, D), row),
            pl.BlockSpec((TM, PLE_DIM), row),
            pl.BlockSpec((D, D), const), pl.BlockSpec((PLE_DIM, D), const),
            pl.BlockSpec((1, D), const), pl.BlockSpec((1, D), const),
        ],
        out_specs=pl.BlockSpec((TM, D), row),
        out_shape=jax.ShapeDtypeStruct((T, D), F32),
        compiler_params=_cparams(("parallel",)), name="ple",
    )(h1, peer_out, p, lw["ple_w_g"], lw["ple_w_p"], lw["ln2_g"], lw["ln2_b"])


def _ssm_params(a_re, a_im, b_re, b_im, c_re, c_im, log_dt):
    G, P, C = SSM_GROUPS, SSM_STATE, SSM_GROUP
    ar, ai = a_re.astype(F32), a_im.astype(F32)
    dt = jnp.exp(log_dt.astype(F32))[..., None]
    mag = jnp.exp(dt * ar)
    abr, abi = mag * jnp.cos(dt * ai), mag * jnp.sin(dt * ai)
    den = ar * ar + ai * ai
    qr = ((abr - 1.0) * ar + abi * ai) / den
    qi = (abi * ar - (abr - 1.0) * ai) / den
    br, bi = b_re.astype(F32), b_im.astype(F32)
    bbr = qr[..., None] * br - qi[..., None] * bi
    bbi = qr[..., None] * bi + qi[..., None] * br
    eye = jnp.eye(G, dtype=F32)

    def bdiag_in(m):
        return jnp.einsum("dgpc,gh->dgchp", m, eye).reshape(2, G * C, G * P)

    def bdiag_out(m):
        return jnp.einsum("dgcp,gh->dgphc", m, eye).reshape(2, G * P, G * C)

    bm = jnp.concatenate([bdiag_in(bbr), bdiag_in(bbi)], axis=2).astype(BF16)
    cm = jnp.concatenate([bdiag_out(c_re.astype(F32)), -bdiag_out(c_im.astype(F32))], axis=1).astype(BF16)
    a1 = (abr.reshape(2, G * P), abi.reshape(2, G * P))
    a2 = (a1[0] * a1[0] - a1[1] * a1[1], 2.0 * a1[0] * a1[1])
    zero = (jnp.zeros_like(a1[0]), jnp.zeros_like(a1[0]))
    pick = lambda z, d: (z[0][d:d + 1], z[1][d:d + 1])

    def rows(top, bottom):
        shape = (top[0].shape[0], SUBLANE // 2, G * P)
        return [jnp.concatenate([jnp.broadcast_to(t[:, None, :], shape),
                                 jnp.broadcast_to(b[:, None, :], shape)], axis=1) for t, b in zip(top, bottom)]

    c8 = jnp.stack([*rows(a1, a1), *rows(zero, zero)], axis=1)
    f1, f2 = rows(pick(a1, 0), pick(a2, 0)), rows(pick(zero, 0), pick(a1, 0))
    b1, b2 = rows(pick(a2, 1), pick(a1, 1)), rows(pick(a1, 1), pick(zero, 1))
    c4 = jnp.concatenate([jnp.stack([*f1, *f2], axis=1), jnp.stack([*b1, *b2], axis=1)], axis=0)
    return bm, cm, c8.astype(F32), c4.astype(F32)


def _rope_tables(L):
    pos = jnp.arange(L, dtype=F32)
    inv_freq = ROPE_THETA ** (-jnp.arange(0, MLA_ROPE, 2, dtype=F32) / MLA_ROPE)
    ang = pos[:, None] * inv_freq[None, :]
    cos, sin = jnp.cos(ang), jnp.sin(ang)
    z = jnp.zeros_like(cos)
    return (jnp.concatenate([cos, cos, z, z], axis=1),
            jnp.concatenate([-sin, z, z, z], axis=1),
            jnp.concatenate([z, sin, z, z], axis=1))


def _layer_weights(i, w_in, conv_w, conv_b, conv_ln_g, conv_ln_b,
                   ssm_a_re, ssm_a_im, ssm_b_re, ssm_b_im, ssm_c_re, ssm_c_im, ssm_log_dt, ssm_d, ssm_glu_w,
                   ssm_glu_b, mla_q_norm_g, mla_w_uq, mla_kv_norm_g, mla_w_ukv, w_out, ln1_g, ln1_b,
                   peer_w_q, peer_k1, peer_k2, peer_u, peer_v, ple_w_p, ple_w_g, ln2_g, ln2_b):
    row = lambda v: v[i].reshape(1, -1).astype(F32)
    H = MLA_HEADS
    wuq = mla_w_uq[i].reshape(MLA_Q_RANK, H, MLA_DK)
    wuq = jnp.pad(wuq, ((0, 0), (0, 0), (0, MLA_DKP - MLA_DK))).reshape(MLA_Q_RANK, H * MLA_DKP)
    wukv = mla_w_ukv[i].reshape(MLA_KV_RANK, H, MLA_NOPE + MLA_V)
    wukv = jnp.concatenate([wukv[:, :, :MLA_NOPE].reshape(MLA_KV_RANK, H * MLA_NOPE),
                            wukv[:, :, MLA_NOPE:].reshape(MLA_KV_RANK, H * MLA_V)], axis=1)
    bm, cm, c8, c4 = _ssm_params(ssm_a_re[i], ssm_a_im[i], ssm_b_re[i], ssm_b_im[i], ssm_c_re[i], ssm_c_im[i],
                                 ssm_log_dt[i])
    return {
        "w_in": jnp.pad(w_in[i], ((0, 0), (0, IN_PAD - w_in.shape[2]))).astype(BF16),
        "q_norm_g": row(mla_q_norm_g), "w_uq": wuq.astype(BF16),
        "kv_norm_g": row(mla_kv_norm_g), "w_ukv": wukv.astype(BF16),
        "conv_w": conv_w[i].astype(F32), "conv_b": row(conv_b),
        "conv_ln_g": row(conv_ln_g), "conv_ln_b": row(conv_ln_b),
        "ssm_bm": bm, "ssm_cm": cm, "ssm_coef8": c8, "ssm_coef4": c4,
        "ssm_d": row(ssm_d), "ssm_glu_w": ssm_glu_w[i].astype(BF16), "ssm_glu_b": row(ssm_glu_b),
        "w_out": w_out[i].astype(BF16), "ln1_g": row(ln1_g), "ln1_b": row(ln1_b),
        "peer_w_q": peer_w_q[i].astype(BF16), "peer_k1": peer_k1[i].astype(F32), "peer_k2": peer_k2[i].astype(F32),
        "peer_u": peer_u[i].astype(BF16), "peer_vt": peer_v[i].T.astype(BF16),
        "ple_w_p": ple_w_p[i].astype(BF16), "ple_w_g": ple_w_g[i].astype(BF16),
        "ln2_g": row(ln2_g), "ln2_b": row(ln2_b),
    }


def _run(x, p, layers, ln_emb):
    B, L, D = x.shape
    assert B in (SUBLANE // 2, SUBLANE), "the S5 scan packs 4 or 8 batch rows per sublane group"
    rope_tabs = _rope_tables(L)
    h = x
    for i, lw in enumerate(layers):
        if i == 0:
            h, zc, zs, q, k, v = _in_proj(h, lw, rope_tabs, ln=ln_emb)
        else:
            zc, zs, q, k, v = _in_proj(h, lw, rope_tabs)
        o_mla = _attention(q, k, v)
        o_conv = _conv(zc, lw)
        yf, yb = _ssm(zs.reshape(L * B, SSM_CH), lw, B)
        h1, h1b, h1t = _mix(h, o_conv, zs, yf.reshape(L, B * SSM_CH), yb.reshape(L, B * SSM_CH), o_mla, lw)
        h1 = h1.reshape(B * L, D)
        h1b = h1b.reshape(B * L, D)
        aux = _route(h1b, lw)
        peer_out = _peer(h1t, aux, lw)
        h = _ple(h1, peer_out, p[i].reshape(B * L, PLE_DIM), lw).reshape(B, L, D)
    return h


def kernel(x_prompt, x_sample, p_prompt, p_sample, ln_emb_g, ln_emb_b, w_in, conv_w, conv_b, conv_ln_g, conv_ln_b, ssm_a_re, ssm_a_im, ssm_b_re, ssm_b_im, ssm_c_re, ssm_c_im, ssm_log_dt, ssm_d, ssm_glu_w, ssm_glu_b, mla_q_norm_g, mla_w_uq, mla_kv_norm_g, mla_w_ukv, w_out, ln1_g, ln1_b, peer_w_q, peer_k1, peer_k2, peer_u, peer_v, ple_w_p, ple_w_g, ln2_g, ln2_b):
    layers = [
        _layer_weights(i, w_in, conv_w, conv_b, conv_ln_g, conv_ln_b,
                       ssm_a_re, ssm_a_im, ssm_b_re, ssm_b_im, ssm_c_re, ssm_c_im, ssm_log_dt, ssm_d, ssm_glu_w,
                       ssm_glu_b, mla_q_norm_g, mla_w_uq, mla_kv_norm_g, mla_w_ukv, w_out, ln1_g, ln1_b,
                       peer_w_q, peer_k1, peer_k2, peer_u, peer_v, ple_w_p, ple_w_g, ln2_g, ln2_b)
        for i in range(DEPTH)
    ]
    ln_emb = (ln_emb_g.reshape(1, -1).astype(F32), ln_emb_b.reshape(1, -1).astype(F32))
    y_prompt = _run(x_prompt, p_prompt, layers, ln_emb)
    y_sample = _run(x_sample, p_sample, layers, ln_emb)
    return (y_prompt, y_sample)
```
